```python
import math
import jax, jax.numpy as jnp
from jax import lax
import numpy as np

D_MODEL = 1024
BATCH = 4
SEQ = 4096
DEPTH = 4

D_MIX = D_MODEL
GROUP_WIDTH = D_MIX // 4
POOL_WINDOWS = (2, 4, 8, 16)
POOL_GROUPS = 4
POOL_DIM = GROUP_WIDTH // POOL_GROUPS
RET_HEADS = 4
RET_HEAD_DIM = GROUP_WIDTH // RET_HEADS
RET_CHUNK = 128
ROPE_BASE = 10000.0
SGU_GROUPS = 4
SGU_DIM = GROUP_WIDTH // SGU_GROUPS
SGU_CHUNK = 128
DN_HEADS = 4
DN_HEAD_DIM = GROUP_WIDTH // DN_HEADS
DN_CONV = 4
DN_CHUNK = 64
D_FF = 2816
FFN_CONV = 3
NORM_EPS = 1e-6

IN_SIZES = (GROUP_WIDTH, GROUP_WIDTH, GROUP_WIDTH, GROUP_WIDTH, GROUP_WIDTH,
            GROUP_WIDTH, GROUP_WIDTH, 3 * GROUP_WIDTH, GROUP_WIDTH, DN_HEADS, DN_HEADS)
IN_SPLITS = tuple(sum(IN_SIZES[:i + 1]) for i in range(len(IN_SIZES) - 1))
P_IN = sum(IN_SIZES)

kernel_name = 'hybrid_parallel_heads_block'


def rms_norm(x, g):
    xf = x.astype(jnp.float32)
    y = xf * lax.rsqrt(jnp.mean(xf * xf, axis=-1, keepdims=True) + NORM_EPS)
    return (y * g.astype(jnp.float32)).astype(x.dtype)


def rms_normalize(xf):
    return xf * lax.rsqrt(jnp.mean(xf * xf, axis=-1, keepdims=True) + NORM_EPS)


def layer_norm(xf, g, b):
    mu = jnp.mean(xf, axis=-1, keepdims=True)
    xc = xf - mu
    var = jnp.mean(xc * xc, axis=-1, keepdims=True)
    return xc * lax.rsqrt(var + NORM_EPS) * g.astype(jnp.float32) + b.astype(jnp.float32)


def l2_normalize(xf):
    return xf * lax.rsqrt(jnp.sum(xf * xf, axis=-1, keepdims=True) + NORM_EPS)


def causal_dwconv(x, w):
    k = w.shape[0]
    rhs = w.astype(x.dtype)[:, None, :]
    return lax.conv_general_dilated(x, rhs, window_strides=(1,), padding=[(k - 1, 0)],
                                    dimension_numbers=('NWC', 'WIO', 'NWC'),
                                    feature_group_count=x.shape[-1])


def rope_tables(seq, dim):
    inv = 1.0 / (ROPE_BASE ** (jnp.arange(0, dim, 2, dtype=jnp.float32) / dim))
    ang = jnp.arange(seq, dtype=jnp.float32)[:, None] * inv[None, :]
    return jnp.cos(ang), jnp.sin(ang)


def apply_rope(x, cos, sin):
    half = x.shape[-1] // 2
    x1, x2 = x[..., :half], x[..., half:]
    c = cos[None, :, None, :]
    s = sin[None, :, None, :]
    return jnp.concatenate([x1 * c - x2 * s, x2 * c + x1 * s], axis=-1)


def pool_mixer(a, pool_w, pool_scale):
    bsz, seq, _ = a.shape
    af = a.astype(jnp.float32).reshape(bsz, seq, POOL_GROUPS, POOL_DIM)
    cs = jnp.cumsum(af, axis=1)
    outs = []
    for gi, win in enumerate(POOL_WINDOWS):
        c = cs[:, :, gi]
        lower = jnp.pad(c[:, :seq - win], ((0, 0), (win, 0), (0, 0)))
        cnt = jnp.minimum(jnp.arange(1, seq + 1), win).astype(jnp.float32)[None, :, None]
        outs.append((c - lower) / cnt - af[:, :, gi])
    d = jnp.stack(outs, axis=2)
    y = jnp.einsum('bsgc,gcd->bsgd', d, pool_w.astype(jnp.float32))
    return y.reshape(bsz, seq, GROUP_WIDTH) * pool_scale.astype(jnp.float32)


def retention(q, k, v):
    bsz, seq, nh, dk = q.shape
    dv = v.shape[-1]
    c = RET_CHUNK
    n = seq // c
    log_gamma = jnp.log(1.0 - 2.0 ** (-5.0 - jnp.arange(nh, dtype=jnp.float32)))
    pos = jnp.arange(c, dtype=jnp.float32)
    diff = pos[:, None] - pos[None, :]
    dmat = jnp.where(diff >= 0, jnp.exp(log_gamma[:, None, None] * jnp.maximum(diff, 0.0)), 0.0)
    qc = q.reshape(bsz, n, c, nh, dk)
    kc = k.reshape(bsz, n, c, nh, dk)
    vc = v.reshape(bsz, n, c, nh, dv)
    scores = jnp.einsum('bnshd,bnthd->bnhst', qc, kc) * dmat
    o_inner = jnp.einsum('bnhst,bnthe->bnshe', scores, vc)
    k_w = jnp.exp(log_gamma[None, :] * (c - 1.0 - pos)[:, None])
    kv = jnp.einsum('bnthd,th,bnthe->nbhde', kc, k_w, vc)
    g_chunk = jnp.exp(log_gamma * c)[None, :, None, None]

    def step(state, kv_n):
        return state * g_chunk + kv_n, state

    _, prev = lax.scan(step, jnp.zeros((bsz, nh, dk, dv), jnp.float32), kv)
    q_w = jnp.exp(log_gamma[None, :] * (pos + 1.0)[:, None])
    o_cross = jnp.einsum('bnshd,sh,nbhde->bnshe', qc, q_w, prev)
    return (o_inner + o_cross).reshape(bsz, seq, nh, dv)


def retention_mixer(q_in, k_in, v_in, g_in, cos, sin):
    bsz, seq, _ = q_in.shape
    shp = (bsz, seq, RET_HEADS, RET_HEAD_DIM)
    q = apply_rope(q_in.astype(jnp.float32).reshape(shp), cos, sin)
    k = apply_rope(k_in.astype(jnp.float32).reshape(shp), cos, sin) * RET_HEAD_DIM ** -0.5
    v = v_in.astype(jnp.float32).reshape(shp)
    o = rms_normalize(retention(q, k, v))
    return o.reshape(bsz, seq, GROUP_WIDTH) * jax.nn.silu(g_in.astype(jnp.float32))


def spatial_gating_mixer(u, v, ln_g, ln_b, ws, bs):
    bsz, seq, _ = u.shape
    n = seq // SGU_CHUNK
    u = jax.nn.gelu(u.astype(jnp.float32), approximate=True)
    v = layer_norm(jax.nn.gelu(v.astype(jnp.float32), approximate=True), ln_g, ln_b)
    vr = v.reshape(bsz, n, SGU_CHUNK, SGU_GROUPS, SGU_DIM)
    mask = jnp.tril(jnp.ones((SGU_CHUNK, SGU_CHUNK), jnp.float32))
    wm = ws.astype(jnp.float32) * mask
    s = jnp.einsum('gts,bnsgc->bntgc', wm, vr) + bs.astype(jnp.float32).T[:, :, None]
    return u * s.reshape(bsz, seq, GROUP_WIDTH)


def gated_delta_rule(q, k, v, g, beta):
    bsz, seq, nh, dk = q.shape
    dv = v.shape[-1]
    c = DN_CHUNK
    n = seq // c

    def chunks(t):
        return jnp.moveaxis(t.reshape((bsz, n, c, nh) + t.shape[3:]), 3, 1)

    qc = chunks(q) * dk ** -0.5
    kc = chunks(k)
    vc = chunks(v)
    bc = chunks(beta)
    decay = jnp.cumsum(chunks(g), axis=-1)
    idx = jnp.arange(c)
    incl = idx[:, None] >= idx[None, :]
    strict = idx[:, None] > idx[None, :]
    diff = decay[..., :, None] - decay[..., None, :]
    dmask = jnp.where(incl, jnp.exp(jnp.where(incl, diff, 0.0)), 0.0)
    k_beta = kc * bc[..., None]
    v_beta = vc * bc[..., None]
    lmat = jnp.where(strict, jnp.einsum('bhnid,bhnjd->bhnij', k_beta, kc) * dmask, 0.0)
    amat = lmat + jnp.eye(c, dtype=jnp.float32)
    rhs = jnp.concatenate([v_beta, k_beta * jnp.exp(decay)[..., None]], axis=-1)
    sol = lax.linalg.triangular_solve(amat, rhs, left_side=True, lower=True, unit_diagonal=True)
    u_c = sol[..., :dv]
    w_c = sol[..., dv:]
    attn = jnp.einsum('bhnid,bhnjd->bhnij', qc, kc) * dmask
    q_dec = qc * jnp.exp(decay)[..., None]
    d_last = decay[..., -1]
    k_tail = kc * jnp.exp(d_last[..., None] - decay)[..., None]

    def step(state, inp):
        u_n, w_n, qd_n, a_n, kt_n, dl_n = inp
        v_new = u_n - jnp.einsum('bhcd,bhde->bhce', w_n, state)
        o_n = jnp.einsum('bhcd,bhde->bhce', qd_n, state) + jnp.einsum('bhij,bhje->bhie', a_n, v_new)
        state = state * jnp.exp(dl_n)[..., None, None] + jnp.einsum('bhcd,bhce->bhde', kt_n, v_new)
        return state, o_n

    xs = tuple(jnp.moveaxis(t, 2, 0) for t in (u_c, w_c, q_dec, attn, k_tail, d_last))
    _, o = lax.scan(step, jnp.zeros((bsz, nh, dk, dv), jnp.float32), xs)
    return jnp.transpose(o, (1, 0, 3, 2, 4)).reshape(bsz, seq, nh, dv)


def deltanet_mixer(qkv_in, z_in, b_in, a_in, conv_w, a_log, dt_bias, norm_g):
    bsz, seq, _ = qkv_in.shape
    shp = (bsz, seq, DN_HEADS, DN_HEAD_DIM)
    qkv = jax.nn.silu(causal_dwconv(qkv_in, conv_w).astype(jnp.float32))
    q, k, v = jnp.split(qkv, 3, axis=-1)
    q = l2_normalize(q.reshape(shp))
    k = l2_normalize(k.reshape(shp))
    v = v.reshape(shp)
    beta = jax.nn.sigmoid(b_in.astype(jnp.float32))
    g = -jnp.exp(a_log.astype(jnp.float32)) * jax.nn.softplus(a_in.astype(jnp.float32) + dt_bias.astype(jnp.float32))
    o = gated_delta_rule(q, k, v, g, beta)
    o = rms_normalize(o) * norm_g.astype(jnp.float32)
    return o.reshape(bsz, seq, GROUP_WIDTH) * jax.nn.silu(z_in.astype(jnp.float32))


def conv_ffn(h, w_up, conv_w, conv_b, w_down):
    a, b = jnp.split(h @ w_up, 2, axis=-1)
    a = causal_dwconv(a, conv_w) + conv_b.astype(a.dtype)
    return (jax.nn.gelu(a, approximate=True) * b) @ w_down


def hybrid_layer(x, g_pre_mix, g_post_mix, g_pre_ffn, g_post_ffn, w_in, pool_w, pool_scale,
                 sgu_ln_g, sgu_ln_b, sgu_ws, sgu_bs, dn_conv_w, dn_a_log, dn_dt_bias, dn_norm_g,
                 w_out, ffn_w_up, ffn_conv_w, ffn_conv_b, ffn_w_down, cos, sin):
    dt = x.dtype
    h = rms_norm(x, g_pre_mix)
    p = h @ w_in
    a_pool, r_q, r_k, r_v, r_g, s_u, s_v, d_qkv, d_z, d_b, d_a = jnp.split(p, IN_SPLITS, axis=-1)
    y_a = pool_mixer(a_pool, pool_w, pool_scale)
    y_b = retention_mixer(r_q, r_k, r_v, r_g, cos, sin)
    y_c = spatial_gating_mixer(s_u, s_v, sgu_ln_g, sgu_ln_b, sgu_ws, sgu_bs)
    y_d = deltanet_mixer(d_qkv, d_z, d_b, d_a, dn_conv_w, dn_a_log, dn_dt_bias, dn_norm_g)
    y = jnp.concatenate([y_a, y_b, y_c, y_d], axis=-1).astype(dt)
    x = x + rms_norm(y @ w_out, g_post_mix)
    h = rms_norm(x, g_pre_ffn)
    x = x + rms_norm(conv_ffn(h, ffn_w_up, ffn_conv_w, ffn_conv_b, ffn_w_down), g_post_ffn)
    return x


def setup_inputs(seed: int = 0) -> dict:
    key = jax.random.key(seed)
    ks = jax.random.split(key, 21)
    L = DEPTH
    D = D_MODEL

    def nrm(k, shape, scale):
        return jax.random.normal(k, shape, jnp.float32) * scale

    dt_init = jnp.exp(jax.random.uniform(ks[14], (L, DN_HEADS), jnp.float32, math.log(1e-3), math.log(1e-1)))
    return {
        'x': nrm(ks[0], (BATCH, SEQ, D), 1.0),
        'norm_pre_mix': 1.0 + nrm(ks[1], (L, D), 0.02),
        'norm_post_mix': 1.0 + nrm(ks[2], (L, D), 0.02),
        'norm_pre_ffn': 1.0 + nrm(ks[3], (L, D), 0.02),
        'norm_post_ffn': 1.0 + nrm(ks[4], (L, D), 0.02),
        'w_in': nrm(ks[5], (L, D, P_IN), D ** -0.5),
        'pool_w': nrm(ks[6], (L, POOL_GROUPS, POOL_DIM, POOL_DIM), POOL_DIM ** -0.5),
        'pool_scale': 1.0 + nrm(ks[7], (L, GROUP_WIDTH), 0.02),
        'sgu_ln_g': 1.0 + nrm(ks[8], (L, GROUP_WIDTH), 0.02),
        'sgu_ln_b': nrm(ks[9], (L, GROUP_WIDTH), 0.02),
        'sgu_ws': nrm(ks[10], (L, SGU_GROUPS, SGU_CHUNK, SGU_CHUNK), SGU_CHUNK ** -0.5),
        'sgu_bs': 1.0 + nrm(ks[11], (L, SGU_GROUPS, SGU_CHUNK), 0.1),
        'dn_conv_w': nrm(ks[12], (L, DN_CONV, 3 * GROUP_WIDTH), DN_CONV ** -0.5),
        'dn_a_log': jnp.log(jax.random.uniform(ks[13], (L, DN_HEADS), jnp.float32, 1.0, 16.0)),
        'dn_dt_bias': dt_init + jnp.log(-jnp.expm1(-dt_init)),
        'dn_norm_g': 1.0 + nrm(ks[15], (L, DN_HEAD_DIM), 0.02),
        'w_out': nrm(ks[16], (L, D_MIX, D), D_MIX ** -0.5),
        'ffn_w_up': nrm(ks[17], (L, D, 2 * D_FF), D ** -0.5),
        'ffn_conv_w': nrm(ks[18], (L, FFN_CONV, D_FF), FFN_CONV ** -0.5),
        'ffn_conv_b': nrm(ks[19], (L, D_FF), 0.02),
        'ffn_w_down': nrm(ks[20], (L, D_FF, D), D_FF ** -0.5),
    }


def reference(x, norm_pre_mix, norm_post_mix, norm_pre_ffn, norm_post_ffn, w_in, pool_w, pool_scale,
              sgu_ln_g, sgu_ln_b, sgu_ws, sgu_bs, dn_conv_w, dn_a_log, dn_dt_bias, dn_norm_g,
              w_out, ffn_w_up, ffn_conv_w, ffn_conv_b, ffn_w_down):
    cos, sin = rope_tables(x.shape[1], RET_HEAD_DIM)
    for l in range(DEPTH):
        x = hybrid_layer(x, norm_pre_mix[l], norm_post_mix[l], norm_pre_ffn[l], norm_post_ffn[l],
                         w_in[l], pool_w[l], pool_scale[l], sgu_ln_g[l], sgu_ln_b[l], sgu_ws[l], sgu_bs[l],
                         dn_conv_w[l], dn_a_log[l], dn_dt_bias[l], dn_norm_g[l], w_out[l],
                         ffn_w_up[l], ffn_conv_w[l], ffn_conv_b[l], ffn_w_down[l], cos, sin)
    return x
```

```python
import functools
import math

import jax
import jax.numpy as jnp
from jax import lax
from jax.experimental import pallas as pl
from jax.experimental.pallas import tpu as pltpu

F32 = jnp.float32
BF16 = jnp.bfloat16

NORM_EPS = 1e-6
GROUP_WIDTH = 256
HEAD_DIM = 64
N_HEADS = 4
POOL_WINDOWS = (2, 4, 8, 16)
POOL_HALO = 16
RET_CHUNK = 128
SGU_CHUNK = 128
DN_CHUNK = 64
DN_CONV = 4
CONV_HALO = 8
ROPE_BASE = 10000.0
FFN_CONV = 3
LANE = 128
P_MAIN = 11 * GROUP_WIDTH
P_PAD = P_MAIN + LANE

MIX_TILE = 128
ROW_TILE = 512
FF_TILE = 256
VMEM_LIMIT = 48 * 1024 * 1024


def _bdot(a, b):
    return jnp.dot(a.astype(BF16), b.astype(BF16), preferred_element_type=F32)


def _bdot_nt(a, b):
    return lax.dot_general(a.astype(BF16), b.astype(BF16), (((1,), (1,)), ((), ())),
                           preferred_element_type=F32)


def _bdot_tn(a, b):
    return lax.dot_general(a.astype(BF16), b.astype(BF16), (((0,), (0,)), ((), ())),
                           preferred_element_type=F32)


def _split_dot(a_bf16, x, terms):
    acc = None
    r = x
    for i in range(terms):
        part = r.astype(BF16)
        d = jnp.dot(a_bf16, part, preferred_element_type=F32)
        acc = d if acc is None else acc + d
        if i + 1 < terms:
            r = r - part.astype(F32)
    return acc


def _split_dot_rhs(x, b_bf16, terms):
    acc = None
    r = x
    for i in range(terms):
        part = r.astype(BF16)
        d = jnp.dot(part, b_bf16, preferred_element_type=F32)
        acc = d if acc is None else acc + d
        if i + 1 < terms:
            r = r - part.astype(F32)
    return acc


def _rms_rows(x):
    return x * lax.rsqrt(jnp.mean(x * x, axis=-1, keepdims=True) + NORM_EPS)


def _softplus(x):
    return jnp.maximum(x, 0.0) + jnp.log1p(jnp.exp(-jnp.abs(x)))


def _gelu(x):
    return jax.nn.gelu(x, approximate=True)


def _silu(x):
    return x * jax.nn.sigmoid(x)


def _in_proj_kernel(x_ref, g_ref, w_ref, p_ref):
    h = (_rms_rows(x_ref[...]) * g_ref[...]).astype(BF16)
    n = p_ref.shape[1]
    for c0 in range(0, n, 512):
        c1 = min(c0 + 512, n)
        p_ref[:, c0:c1] = jnp.dot(h, w_ref[:, c0:c1], preferred_element_type=F32)


def _in_proj(x2d, g, w_bf16):
    rows, d = x2d.shape
    n = w_bf16.shape[1]
    tm = min(ROW_TILE, rows)
    return pl.pallas_call(
        _in_proj_kernel,
        grid=(rows // tm,),
        in_specs=[pl.BlockSpec((tm, d), lambda i: (i, 0)),
                  pl.BlockSpec((1, d), lambda i: (0, 0)),
                  pl.BlockSpec((d, n), lambda i: (0, 0))],
        out_specs=pl.BlockSpec((tm, n), lambda i: (i, 0)),
        out_shape=jax.ShapeDtypeStruct((rows, n), F32),
        compiler_params=pltpu.CompilerParams(dimension_semantics=("arbitrary",),
                                             vmem_limit_bytes=VMEM_LIMIT),
        name="in_proj",
    )(x2d, g, w_bf16)


def _mix_kernel(p_ref, cos_ref, sin_ref, dmat_ref, kw_ref, qw_ref, gc_ref,
                poolw_ref, pools_ref, lng_ref, lnb_ref, ws_ref, bs_ref,
                convw_ref, alog_ref, dtb_ref, dng_ref,
                y_ref,
                pool_ext, conv_ext, s_ret, s_dn):
    t = MIX_TILE
    gw = GROUP_WIDTH
    first_tile = pl.program_id(1) == 0

    @pl.when(first_tile)
    def _():
        pool_ext[0:POOL_HALO, :] = jnp.zeros((POOL_HALO, gw), F32)
        conv_ext[0:CONV_HALO, :] = jnp.zeros((CONV_HALO, 3 * gw), F32)
        s_ret[...] = jnp.zeros((gw, gw), F32)
        s_dn[...] = jnp.zeros((gw, gw), F32)

    lane = lax.broadcasted_iota(jnp.int32, (t, gw), 1)
    head_of_lane = lane >> 6
    head_masks = [head_of_lane == h for h in range(N_HEADS)]
    row_t = lax.broadcasted_iota(jnp.int32, (t, gw), 0)
    sr = lax.broadcasted_iota(jnp.int32, (gw, gw), 0) >> 6
    sc = lax.broadcasted_iota(jnp.int32, (gw, gw), 1) >> 6
    state_mask = sr == sc
    head_ones = jnp.where(state_mask, 1.0, 0.0).astype(BF16)

    def head_sum(x):
        return _split_dot_rhs(x, head_ones, 2)

    a = p_ref[:, 0:gw]
    pool_ext[POOL_HALO:POOL_HALO + t, :] = a
    run = a
    sums = []
    for k in range(1, POOL_HALO):
        run = run + pool_ext[pl.ds(POOL_HALO - k, t), :]
        if k + 1 in POOL_WINDOWS:
            sums.append(run)
    win_sum = sums[3]
    win = jnp.full((t, gw), float(POOL_WINDOWS[3]), F32)
    for gi in (2, 1, 0):
        win_sum = jnp.where(head_masks[gi], sums[gi], win_sum)
        win = jnp.where(head_masks[gi], float(POOL_WINDOWS[gi]), win)
    pos = (pl.program_id(1) * t + row_t + 1).astype(F32)
    d_pool = win_sum / jnp.minimum(pos, win) - a
    pool_ext[0:POOL_HALO, :] = a[t - POOL_HALO:, :]
    y_a = _bdot(d_pool, poolw_ref[...]) * pools_ref[...]
    y_ref[:, 0:gw] = y_a.astype(y_ref.dtype)

    cos = cos_ref[...]
    sin = sin_ref[...]
    lane128 = lax.broadcasted_iota(jnp.int32, (t, LANE), 1)
    first_half = (lane128 & (HEAD_DIM - 1)) < (HEAD_DIM // 2)

    def rope(xh):
        partner = jnp.where(first_half, pltpu.roll(xh, LANE - HEAD_DIM // 2, 1),
                            pltpu.roll(xh, HEAD_DIM // 2, 1))
        return xh * cos + partner * sin

    def rope2(x):
        return jnp.concatenate([rope(x[:, :LANE]), rope(x[:, LANE:])], axis=1)

    rq = rope2(p_ref[:, gw:2 * gw])
    rk = rope2(p_ref[:, 2 * gw:3 * gw]) * (HEAD_DIM ** -0.5)
    rv = p_ref[:, 3 * gw:4 * gw]
    rkb = rk.astype(BF16)
    rvb = rv.astype(BF16)
    o_ret = _bdot(rq * qw_ref[...], s_ret[...])
    for h in range(N_HEADS):
        scores = _bdot_nt(jnp.where(head_masks[h], rq, 0.0), rkb) * dmat_ref[h]
        o_ret = o_ret + jnp.where(head_masks[h], _bdot(scores, rvb), 0.0)
    kv = _bdot_tn(rk * kw_ref[...], rvb)
    s_ret[...] = s_ret[...] * gc_ref[...] + jnp.where(state_mask, kv, 0.0)
    o_ret = o_ret * lax.rsqrt(head_sum(o_ret * o_ret) * (1.0 / HEAD_DIM) + NORM_EPS)
    y_ref[:, gw:2 * gw] = (o_ret * _silu(p_ref[:, 4 * gw:5 * gw])).astype(y_ref.dtype)

    su = _gelu(p_ref[:, 5 * gw:6 * gw])
    sv = _gelu(p_ref[:, 6 * gw:7 * gw])
    mu = jnp.mean(sv, axis=-1, keepdims=True)
    svc = sv - mu
    var = jnp.mean(svc * svc, axis=-1, keepdims=True)
    svn = (svc * lax.rsqrt(var + NORM_EPS) * lng_ref[...] + lnb_ref[...]).astype(BF16)
    rr = lax.broadcasted_iota(jnp.int32, (t, t), 0)
    cc = lax.broadcasted_iota(jnp.int32, (t, t), 1)
    tril = rr >= cc
    s_gate = bs_ref[...]
    for g in range(N_HEADS):
        wm = jnp.where(tril, ws_ref[g], 0.0)
        s_gate = s_gate + jnp.where(head_masks[g], _bdot(wm, svn), 0.0)
    y_ref[:, 2 * gw:3 * gw] = (su * s_gate).astype(y_ref.dtype)

    conv_ext[CONV_HALO:CONV_HALO + t, :] = p_ref[:, 7 * gw:10 * gw]
    qkv = None
    for j in range(DN_CONV):
        term = conv_ext[pl.ds(CONV_HALO - (DN_CONV - 1) + j, t), :] * convw_ref[j:j + 1, :]
        qkv = term if qkv is None else qkv + term
    conv_ext[0:CONV_HALO, :] = conv_ext[t:t + CONV_HALO, :]
    qkv = _silu(qkv)
    dq = qkv[:, 0:gw]
    dk = qkv[:, gw:2 * gw]
    dv = qkv[:, 2 * gw:3 * gw]
    dq = dq * lax.rsqrt(head_sum(dq * dq) + NORM_EPS) * (HEAD_DIM ** -0.5)
    dk = dk * lax.rsqrt(head_sum(dk * dk) + NORM_EPS)

    ba = p_ref[:, P_MAIN:P_PAD]
    beta_c = jax.nn.sigmoid(ba)
    g_c = -jnp.exp(alog_ref[...]) * _softplus(ba + dtb_ref[...])

    def expand(x, base):
        out = jnp.broadcast_to(x[:, base + 3:base + 4], (t, gw))
        for h in (2, 1, 0):
            out = jnp.where(head_masks[h], jnp.broadcast_to(x[:, base + h:base + h + 1], (t, gw)), out)
        return out

    beta = expand(beta_c, 0)
    g_l = expand(g_c, N_HEADS)

    same_chunk = (rr >> 6) == (cc >> 6)
    incl = same_chunk & (rr >= cc)
    strict = same_chunk & (rr > cc)
    tril_bd = jnp.where(incl, 1.0, 0.0).astype(BF16)
    eye = jnp.where(rr == cc, 1.0, 0.0)

    decay = _split_dot(tril_bd, g_l, 3)
    exp_decay = jnp.exp(decay)
    k_beta = dk * beta
    v_beta = dv * beta
    rhs = jnp.concatenate([v_beta, k_beta * exp_decay], axis=1).astype(BF16)
    q_dec = dq * exp_decay
    n_chunks = t // DN_CHUNK
    d_last = [decay[(c + 1) * DN_CHUNK - 1:(c + 1) * DN_CHUNK, :] for c in range(n_chunks)]
    d_last_rows = jnp.concatenate([jnp.broadcast_to(dl, (DN_CHUNK, gw)) for dl in d_last], axis=0)
    k_tail = dk * jnp.exp(d_last_rows - decay)
    dkb = dk.astype(BF16)

    u = jnp.zeros((t, gw), F32)
    w = jnp.zeros((t, gw), F32)
    attn = []
    for h in range(N_HEADS):
        g_col = jnp.broadcast_to(g_c[:, N_HEADS + h:N_HEADS + h + 1], (t, t))
        diff = _split_dot(tril_bd, jnp.where(strict, g_col, 0.0), 3)
        dmask = jnp.where(incl, jnp.exp(diff), 0.0)
        lmat = jnp.where(strict, _bdot_nt(jnp.where(head_masks[h], k_beta, 0.0), dkb) * dmask, 0.0)
        attn.append(_bdot_nt(jnp.where(head_masks[h], dq, 0.0), dkb) * dmask)
        inv = eye - lmat
        power = lmat
        for _ in range(5):
            power = _bdot(power, power)
            inv = inv + _bdot(inv, power)
        sol = _bdot(inv, rhs)
        u = u + jnp.where(head_masks[h], sol[:, 0:gw], 0.0)
        w = w + jnp.where(head_masks[h], sol[:, gw:2 * gw], 0.0)

    state = s_dn[...]
    v_new = []
    o_cross = []
    for c in range(n_chunks):
        r0, r1 = c * DN_CHUNK, (c + 1) * DN_CHUNK
        sb = state.astype(BF16)
        vn = u[r0:r1] - _bdot(w[r0:r1], sb)
        o_cross.append(_bdot(q_dec[r0:r1], sb))
        state = state * jnp.exp(d_last[c]) + jnp.where(state_mask, _bdot_tn(k_tail[r0:r1], vn), 0.0)
        v_new.append(vn)
    s_dn[...] = state
    v_all = jnp.concatenate(v_new, axis=0).astype(BF16)
    o_dn = jnp.concatenate(o_cross, axis=0)
    for h in range(N_HEADS):
        o_dn = o_dn + jnp.where(head_masks[h], _bdot(attn[h], v_all), 0.0)
    o_dn = o_dn * lax.rsqrt(head_sum(o_dn * o_dn) * (1.0 / HEAD_DIM) + NORM_EPS) * dng_ref[...]
    y_ref[:, 3 * gw:4 * gw] = (o_dn * _silu(p_ref[:, 10 * gw:11 * gw])).astype(y_ref.dtype)


def _mixers(p, batch, seq, consts, lp):
    t = MIX_TILE
    nt = seq // t
    gw = GROUP_WIDTH

    def full(arr):
        nd = arr.ndim
        return pl.BlockSpec(arr.shape, lambda b, i: (0,) * nd)

    const_inputs = [consts["dmat"], consts["kw"], consts["qw"], consts["gc"]]
    layer_inputs = [lp["pool_w"], lp["pool_scale"], lp["ln_g"], lp["ln_b"], lp["ws"], lp["bs"],
                    lp["conv_w"], lp["a_log"], lp["dt_bias"], lp["dn_g"]]
    in_specs = ([pl.BlockSpec((t, P_PAD), lambda b, i: (b * nt + i, 0)),
                 pl.BlockSpec((t, LANE), lambda b, i: (i, 0)),
                 pl.BlockSpec((t, LANE), lambda b, i: (i, 0))]
                + [full(a) for a in const_inputs] + [full(a) for a in layer_inputs])
    return pl.pallas_call(
        _mix_kernel,
        grid=(batch, nt),
        in_specs=in_specs,
        out_specs=pl.BlockSpec((t, 4 * gw), lambda b, i: (b * nt + i, 0)),
        out_shape=jax.ShapeDtypeStruct((batch * seq, 4 * gw), BF16),
        scratch_shapes=[pltpu.VMEM((POOL_HALO + t, gw), F32),
                        pltpu.VMEM((CONV_HALO + t, 3 * gw), F32),
                        pltpu.VMEM((gw, gw), F32),
                        pltpu.VMEM((gw, gw), F32)],
        compiler_params=pltpu.CompilerParams(dimension_semantics=("arbitrary", "arbitrary"),
                                             vmem_limit_bytes=VMEM_LIMIT),
        name="mixers",
    )(p, consts["cos"], consts["sin"], *const_inputs, *layer_inputs)


def _out_proj_kernel(y_ref, x_ref, w_ref, g_ref, o_ref):
    z = jnp.dot(y_ref[...], w_ref[...], preferred_element_type=F32)
    o_ref[...] = x_ref[...] + _rms_rows(z) * g_ref[...]


def _out_proj(y, x2d, w_bf16, g):
    rows, d = x2d.shape
    tm = min(ROW_TILE, rows)
    return pl.pallas_call(
        _out_proj_kernel,
        grid=(rows // tm,),
        in_specs=[pl.BlockSpec((tm, y.shape[1]), lambda i: (i, 0)),
                  pl.BlockSpec((tm, d), lambda i: (i, 0)),
                  pl.BlockSpec(w_bf16.shape, lambda i: (0, 0)),
                  pl.BlockSpec((1, d), lambda i: (0, 0))],
        out_specs=pl.BlockSpec((tm, d), lambda i: (i, 0)),
        out_shape=jax.ShapeDtypeStruct((rows, d), F32),
        compiler_params=pltpu.CompilerParams(dimension_semantics=("arbitrary",),
                                             vmem_limit_bytes=VMEM_LIMIT),
        name="out_proj",
    )(y, x2d, w_bf16, g)


def _ffn_kernel(x_ref, gpre_ref, wup_ref, cw_ref, cb_ref, wdn_ref, gpost_ref, o_ref,
                h_ref, acc_ref, ext_ref, carry_ref, *, tiles_per_seq):
    i = pl.program_id(0)
    j = pl.program_id(1)
    tm = x_ref.shape[0]

    @pl.when(j == 0)
    def _():
        h_ref[...] = (_rms_rows(x_ref[...]) * gpre_ref[...]).astype(BF16)
        acc_ref[...] = jnp.zeros_like(acc_ref)

    h = h_ref[...]
    a = jnp.dot(h, wup_ref[0, 0], preferred_element_type=F32)
    b = jnp.dot(h, wup_ref[0, 1], preferred_element_type=F32)

    seq_start = (i % tiles_per_seq) == 0

    @pl.when(seq_start)
    def _():
        ext_ref[0:CONV_HALO, :] = jnp.zeros((CONV_HALO, a.shape[1]), F32)

    @pl.when(jnp.logical_not(seq_start))
    def _():
        ext_ref[0:CONV_HALO, :] = carry_ref[j]

    ext_ref[CONV_HALO:CONV_HALO + tm, :] = a
    carry_ref[j] = a[tm - CONV_HALO:, :]
    conv = a * cw_ref[FFN_CONV - 1:FFN_CONV, :] + cb_ref[...]
    for k in range(1, FFN_CONV):
        conv = conv + ext_ref[pl.ds(CONV_HALO - k, tm), :] * cw_ref[FFN_CONV - 1 - k:FFN_CONV - k, :]
    gated = (_gelu(conv) * b).astype(BF16)
    acc_ref[...] += jnp.dot(gated, wdn_ref[...], preferred_element_type=F32)

    @pl.when(j == pl.num_programs(1) - 1)
    def _():
        o_ref[...] = x_ref[...] + _rms_rows(acc_ref[...]) * gpost_ref[...]


def _ffn(x2d, seq, g_pre, w_up_r, conv_w, conv_b, w_down, g_post):
    rows, d = x2d.shape
    nf, _, _, tf = w_up_r.shape
    tm = min(ROW_TILE, rows)
    return pl.pallas_call(
        functools.partial(_ffn_kernel, tiles_per_seq=seq // tm),
        grid=(rows // tm, nf),
        in_specs=[pl.BlockSpec((tm, d), lambda i, j: (i, 0)),
                  pl.BlockSpec((1, d), lambda i, j: (0, 0)),
                  pl.BlockSpec((1, 2, d, tf), lambda i, j: (j, 0, 0, 0)),
                  pl.BlockSpec((FFN_CONV, tf), lambda i, j: (0, j)),
                  pl.BlockSpec((1, tf), lambda i, j: (0, j)),
                  pl.BlockSpec((tf, d), lambda i, j: (j, 0)),
                  pl.BlockSpec((1, d), lambda i, j: (0, 0))],
        out_specs=pl.BlockSpec((tm, d), lambda i, j: (i, 0)),
        out_shape=jax.ShapeDtypeStruct((rows, d), F32),
        scratch_shapes=[pltpu.VMEM((tm, d), BF16),
                        pltpu.VMEM((tm, d), F32),
                        pltpu.VMEM((CONV_HALO + tm, tf), F32),
                        pltpu.VMEM((nf, CONV_HALO, tf), F32)],
        compiler_params=pltpu.CompilerParams(dimension_semantics=("arbitrary", "arbitrary"),
                                             vmem_limit_bytes=VMEM_LIMIT),
        name="conv_ffn",
    )(x2d, g_pre, w_up_r, conv_w, conv_b, w_down, g_post)


def _shape_constants(seq):
    half = HEAD_DIM // 2
    inv = 1.0 / (ROPE_BASE ** (jnp.arange(0, HEAD_DIM, 2, dtype=F32) / HEAD_DIM))
    ang = jnp.arange(seq, dtype=F32)[:, None] * inv[None, :]
    cos, sin = jnp.cos(ang), jnp.sin(ang)
    cos_l = jnp.tile(cos, (1, LANE // half))
    sin_l = jnp.tile(jnp.concatenate([-sin, sin], axis=1), (1, LANE // HEAD_DIM))
    c = RET_CHUNK
    log_gamma = jnp.log(1.0 - 2.0 ** (-5.0 - jnp.arange(N_HEADS, dtype=F32)))
    pos = jnp.arange(c, dtype=F32)
    diff = pos[:, None] - pos[None, :]
    dmat = jnp.where(diff >= 0, jnp.exp(log_gamma[:, None, None] * jnp.maximum(diff, 0.0)), 0.0)
    k_w = jnp.exp(log_gamma[None, :] * (c - 1.0 - pos)[:, None])
    q_w = jnp.exp(log_gamma[None, :] * (pos + 1.0)[:, None])
    g_chunk = jnp.exp(log_gamma * c)[None, :]
    rep = lambda m: jnp.repeat(m, HEAD_DIM, axis=1)
    return {"cos": cos_l, "sin": sin_l, "dmat": dmat, "kw": rep(k_w), "qw": rep(q_w), "gc": rep(g_chunk)}


def _layer_params(l, w_in, pool_w, pool_scale, sgu_ln_g, sgu_ln_b, sgu_ws, sgu_bs, dn_conv_w,
                  dn_a_log, dn_dt_bias, dn_norm_g):
    d = w_in.shape[1]
    w_in_p = jnp.pad(w_in[l], ((0, 0), (0, P_PAD - w_in.shape[2]))).astype(BF16)
    pool_bd = jnp.zeros((GROUP_WIDTH, GROUP_WIDTH), F32)
    for g in range(N_HEADS):
        pool_bd = pool_bd.at[g * HEAD_DIM:(g + 1) * HEAD_DIM, g * HEAD_DIM:(g + 1) * HEAD_DIM].set(pool_w[l, g])
    gate_pad = jnp.zeros((1, LANE), F32)
    return {
        "w_in": w_in_p,
        "pool_w": pool_bd.astype(BF16),
        "pool_scale": pool_scale[l][None, :],
        "ln_g": sgu_ln_g[l][None, :],
        "ln_b": sgu_ln_b[l][None, :],
        "ws": sgu_ws[l],
        "bs": jnp.repeat(sgu_bs[l].T, HEAD_DIM, axis=1),
        "conv_w": dn_conv_w[l],
        "a_log": gate_pad.at[0, N_HEADS:2 * N_HEADS].set(dn_a_log[l]),
        "dt_bias": gate_pad.at[0, N_HEADS:2 * N_HEADS].set(dn_dt_bias[l]),
        "dn_g": jnp.tile(dn_norm_g[l], N_HEADS)[None, :],
    }


def kernel(x, norm_pre_mix, norm_post_mix, norm_pre_ffn, norm_post_ffn, w_in, pool_w, pool_scale, sgu_ln_g, sgu_ln_b, sgu_ws, sgu_bs, dn_conv_w, dn_a_log, dn_dt_bias, dn_norm_g, w_out, ffn_w_up, ffn_conv_w, ffn_conv_b, ffn_w_down):
    batch, seq, d = x.shape
    depth = w_in.shape[0]
    d_ff = ffn_w_down.shape[1]
    nf = d_ff // FF_TILE
    consts = _shape_constants(seq)
    x2d = x.reshape(batch * seq, d)
    for l in range(depth):
        lp = _layer_params(l, w_in, pool_w, pool_scale, sgu_ln_g, sgu_ln_b, sgu_ws, sgu_bs,
                           dn_conv_w, dn_a_log, dn_dt_bias, dn_norm_g)
        p = _in_proj(x2d, norm_pre_mix[l][None, :], lp["w_in"])
        y = _mixers(p, batch, seq, consts, lp)
        x2d = _out_proj(y, x2d, w_out[l].astype(BF16), norm_post_mix[l][None, :])
        w_up_r = ffn_w_up[l].astype(BF16).reshape(d, 2, nf, FF_TILE).transpose(2, 1, 0, 3)
        x2d = _ffn(x2d, seq, norm_pre_ffn[l][None, :], w_up_r, ffn_conv_w[l], ffn_conv_b[l][None, :],
                   ffn_w_down[l].astype(BF16), norm_post_ffn[l][None, :])
    return x2d.reshape(batch, seq, d)
```

```python
import functools

import jax
import jax.numpy as jnp
from jax import lax
from jax.experimental import pallas as pl
from jax.experimental.pallas import tpu as pltpu

F32 = jnp.float32
BF16 = jnp.bfloat16

NORM_EPS = 1e-6
GROUP_WIDTH = 256
HEAD_DIM = 64
N_HEADS = 4
POOL_WINDOWS = (2, 4, 8, 16)
POOL_HALO = 32
RET_CHUNK = 128
SGU_CHUNK = 128
DN_CHUNK = 64
DN_CONV = 4
CONV_HALO = 8
ROPE_BASE = 10000.0
FFN_CONV = 3
LANE = 128
SUBLANE = 8
P_MAIN = 11 * GROUP_WIDTH
P_PAD = P_MAIN + LANE

MIX_TILE = 128
ROW_TILE = 512
FF_TILE = 256
VMEM_LIMIT = 48 * 1024 * 1024


def _bdot(a, b):
    return jnp.dot(a.astype(BF16), b.astype(BF16), preferred_element_type=F32)


def _bdot_nt(a, b):
    return lax.dot_general(a.astype(BF16), b.astype(BF16), (((1,), (1,)), ((), ())),
                           preferred_element_type=F32)


def _bdot_tn(a, b):
    return lax.dot_general(a.astype(BF16), b.astype(BF16), (((0,), (0,)), ((), ())),
                           preferred_element_type=F32)


def _split(x, terms):
    parts = []
    r = x
    for i in range(terms):
        part = r.astype(BF16)
        parts.append(part)
        if i + 1 < terms:
            r = r - part.astype(F32)
    return parts


def _split_dot(a_bf16, x, terms):
    return sum(jnp.dot(a_bf16, part, preferred_element_type=F32) for part in _split(x, terms))


def _split_dot_rhs(x, b_bf16, terms):
    return sum(jnp.dot(part, b_bf16, preferred_element_type=F32) for part in _split(x, terms))


def _rms_rows(x):
    return x * lax.rsqrt(jnp.mean(x * x, axis=-1, keepdims=True) + NORM_EPS)


def _softplus(x):
    return jnp.maximum(x, 0.0) + jnp.log1p(jnp.exp(-jnp.abs(x)))


def _gelu(x):
    return jax.nn.gelu(x, approximate=True)


def _silu(x):
    return x * jax.nn.sigmoid(x)


def _resident(shape):
    nd = len(shape)
    return pl.BlockSpec(shape, lambda *_: (0,) * nd, pipeline_mode=pl.Buffered(1))


def _in_proj_kernel(x_ref, g_ref, w_ref, p_ref):
    h = (_rms_rows(x_ref[...]) * g_ref[...]).astype(BF16)
    n = p_ref.shape[1]
    for c0 in range(0, n, 512):
        c1 = min(c0 + 512, n)
        p_ref[:, c0:c1] = jnp.dot(h, w_ref[:, c0:c1], preferred_element_type=F32)


def _in_proj(x2d, g, w_bf16):
    rows, d = x2d.shape
    n = w_bf16.shape[1]
    tm = min(ROW_TILE, rows)
    return pl.pallas_call(
        _in_proj_kernel,
        grid=(rows // tm,),
        in_specs=[pl.BlockSpec((tm, d), lambda i: (i, 0)),
                  _resident((1, d)),
                  _resident((d, n))],
        out_specs=pl.BlockSpec((tm, n), lambda i: (i, 0)),
        out_shape=jax.ShapeDtypeStruct((rows, n), F32),
        compiler_params=pltpu.CompilerParams(dimension_semantics=("arbitrary",),
                                             vmem_limit_bytes=VMEM_LIMIT),
        name="in_proj",
    )(x2d, g, w_bf16)


def _mix_kernel(p_ref, cos_ref, sin_ref, dmat_ref, kw_ref, qw_ref, gc_ref, expand_ref,
                poolw_ref, pools_ref, lng_ref, lnb_ref, ws_ref, bs_ref,
                convw_ref, alog_ref, dtb_ref, dng_ref,
                y_ref,
                pool_ext, pool_s2, pool_s4, pool_s8, conv_ext, s_ret, s_dn):
    t = MIX_TILE
    gw = GROUP_WIDTH
    nh = N_HEADS

    @pl.when(pl.program_id(1) == 0)
    def _():
        pool_ext[0:POOL_HALO, :] = jnp.zeros((POOL_HALO, gw), F32)
        conv_ext[0:CONV_HALO, :] = jnp.zeros((CONV_HALO, 3 * gw), F32)
        s_ret[...] = jnp.zeros((gw, gw), F32)
        s_dn[...] = jnp.zeros((gw, gw), F32)

    lane = lax.broadcasted_iota(jnp.int32, (t, gw), 1)
    head_of_lane = lane >> 6
    mb = [jnp.where(head_of_lane == h, 1.0, 0.0).astype(BF16) for h in range(nh)]
    row_t = lax.broadcasted_iota(jnp.int32, (t, gw), 0)
    lane128 = lax.broadcasted_iota(jnp.int32, (t, LANE), 1)
    sr = lax.broadcasted_iota(jnp.int32, (gw, gw), 0) >> 6
    sc = lax.broadcasted_iota(jnp.int32, (gw, gw), 1) >> 6
    state_mask = sr == sc
    head_ones = jnp.where(state_mask, 1.0, 0.0).astype(BF16)
    rr = lax.broadcasted_iota(jnp.int32, (t, t), 0)
    cc = lax.broadcasted_iota(jnp.int32, (t, t), 1)

    def head_sum(x):
        return _split_dot_rhs(x, head_ones, 2)

    def head_rows(x_bf16):
        reps = x_bf16.shape[1] // gw
        return jnp.concatenate(
            [x_bf16 * (mb[h] if reps == 1 else jnp.concatenate([mb[h]] * reps, axis=1))
             for h in range(nh)], axis=0)

    def side_by_side(blocks):
        return jnp.concatenate([b.astype(BF16) for b in blocks], axis=1)

    conv_ext[CONV_HALO:CONV_HALO + t, :] = p_ref[:, 7 * gw:10 * gw]
    qkv = None
    for j in range(DN_CONV):
        term = conv_ext[pl.ds(CONV_HALO - (DN_CONV - 1) + j, t), :] * convw_ref[j:j + 1, :]
        qkv = term if qkv is None else qkv + term
    conv_ext[0:CONV_HALO, :] = conv_ext[t:t + CONV_HALO, :]
    qkv = _silu(qkv)
    dq = qkv[:, 0:gw]
    dk = qkv[:, gw:2 * gw]
    dv = qkv[:, 2 * gw:3 * gw]
    dq = dq * lax.rsqrt(head_sum(dq * dq) + NORM_EPS) * (HEAD_DIM ** -0.5)
    dk = dk * lax.rsqrt(head_sum(dk * dk) + NORM_EPS)

    same_chunk = (rr >> 6) == (cc >> 6)
    incl = same_chunk & (rr >= cc)
    strict = same_chunk & (rr > cc)
    tril_bd = jnp.where(incl, 1.0, 0.0).astype(BF16)
    eye = jnp.where(rr == cc, 1.0, 0.0)

    ba = p_ref[:, P_MAIN:P_PAD]
    g_c = -jnp.exp(alog_ref[...]) * _softplus(ba + dtb_ref[...])
    decay_c = _split_dot(tril_bd, g_c, 3)
    gates = jnp.where(lane128 < nh, jax.nn.sigmoid(ba), decay_c)
    gates = _split_dot_rhs(gates, expand_ref[...], 3)
    beta = gates[:, 0:gw]
    decay = gates[:, gw:2 * gw]
    decay_rows = decay_c.T

    exp_decay = jnp.exp(decay)
    k_beta = dk * beta
    rhs = jnp.concatenate([dv * beta, k_beta * exp_decay], axis=1).astype(BF16)
    q_dec = dq * exp_decay
    n_chunks = t // DN_CHUNK
    d_last = [decay[(c + 1) * DN_CHUNK - 1:(c + 1) * DN_CHUNK, :] for c in range(n_chunks)]
    d_last_rows = jnp.concatenate([jnp.broadcast_to(dl, (DN_CHUNK, gw)) for dl in d_last], axis=0)
    k_tail = dk * jnp.exp(d_last_rows - decay)
    dkb = dk.astype(BF16)

    kq = _bdot_nt(jnp.concatenate([head_rows(k_beta.astype(BF16)), head_rows(dq.astype(BF16))], axis=0),
                  dkb)
    power = []
    attn = []
    for h in range(nh):
        col = jnp.broadcast_to(decay_c[:, nh + h:nh + h + 1], (t, t))
        row = jnp.broadcast_to(decay_rows[nh + h:nh + h + 1, :], (t, t))
        dmask = jnp.where(incl, jnp.exp(col - row), 0.0)
        power.append(jnp.where(strict, kq[h * t:(h + 1) * t] * dmask, 0.0))
        attn.append(kq[(nh + h) * t:(nh + h + 1) * t] * dmask)
    inv = [eye - lmat for lmat in power]
    for _ in range(5):
        power = [_bdot(pw, pw) for pw in power]
        inv = [iv + _bdot(iv, pw) for iv, pw in zip(inv, power)]
    sol = jnp.dot(side_by_side(inv), head_rows(rhs), preferred_element_type=F32)
    u = sol[:, 0:gw]
    w = sol[:, gw:2 * gw]

    a = p_ref[:, 0:gw]
    n = POOL_HALO + t
    pool_ext[POOL_HALO:n, :] = a
    s2 = pool_ext[8:n, :] + pool_ext[pl.ds(7, n - 8), :]
    pool_s2[8:n, :] = s2
    s4 = pool_s2[16:n, :] + pool_s2[pl.ds(14, n - 16), :]
    pool_s4[16:n, :] = s4
    s8 = pool_s4[24:n, LANE:] + pool_s4[pl.ds(20, n - 24), LANE:]
    pool_s8[24:n, :] = s8
    s16 = pool_s8[32:n, :] + pool_s8[24:n - 8, :]
    pool_ext[0:POOL_HALO, :] = pool_ext[t:n, :]
    low = lane128 < HEAD_DIM
    win_sum = jnp.concatenate([jnp.where(low, s2[24:, 0:LANE], s4[16:, 0:LANE]),
                               jnp.where(low, s8[8:, :], s16)], axis=1)
    win = jnp.concatenate([jnp.where(low, float(POOL_WINDOWS[0]), float(POOL_WINDOWS[1])),
                           jnp.where(low, float(POOL_WINDOWS[2]), float(POOL_WINDOWS[3]))], axis=1)
    pos = (pl.program_id(1) * t + row_t + 1).astype(F32)
    d_pool = win_sum / jnp.minimum(pos, win) - a
    y_a = _bdot(d_pool, poolw_ref[...]) * pools_ref[...]
    y_ref[:, 0:gw] = y_a.astype(y_ref.dtype)

    state = s_dn[...]
    v_new = []
    o_cross = []
    for c in range(n_chunks):
        r0, r1 = c * DN_CHUNK, (c + 1) * DN_CHUNK
        sb = state.astype(BF16)
        vn = u[r0:r1] - _bdot(w[r0:r1], sb)
        o_cross.append(_bdot(q_dec[r0:r1], sb))
        state = state * jnp.exp(d_last[c]) + jnp.where(state_mask, _bdot_tn(k_tail[r0:r1], vn), 0.0)
        v_new.append(vn)
    s_dn[...] = state

    cos = cos_ref[...]
    sin = sin_ref[...]
    first_half = (lane128 & (HEAD_DIM - 1)) < (HEAD_DIM // 2)

    def rope(xh):
        partner = jnp.where(first_half, pltpu.roll(xh, LANE - HEAD_DIM // 2, 1),
                            pltpu.roll(xh, HEAD_DIM // 2, 1))
        return xh * cos + partner * sin

    def rope2(x):
        return jnp.concatenate([rope(x[:, :LANE]), rope(x[:, LANE:])], axis=1)

    rq = rope2(p_ref[:, gw:2 * gw])
    rk = rope2(p_ref[:, 2 * gw:3 * gw]) * (HEAD_DIM ** -0.5)
    rkb = rk.astype(BF16)
    rvb = p_ref[:, 3 * gw:4 * gw].astype(BF16)
    scores = _bdot_nt(head_rows(rq.astype(BF16)), rkb) * dmat_ref[...]
    o_ret = (_bdot(rq * qw_ref[...], s_ret[...])
             + jnp.dot(side_by_side([scores[h * t:(h + 1) * t] for h in range(nh)]), head_rows(rvb),
                       preferred_element_type=F32))
    kv = _bdot_tn(rk * kw_ref[...], rvb)
    s_ret[...] = s_ret[...] * gc_ref[...] + jnp.where(state_mask, kv, 0.0)
    o_ret = o_ret * lax.rsqrt(head_sum(o_ret * o_ret) * (1.0 / HEAD_DIM) + NORM_EPS)
    y_ref[:, gw:2 * gw] = (o_ret * _silu(p_ref[:, 4 * gw:5 * gw])).astype(y_ref.dtype)

    su = _gelu(p_ref[:, 5 * gw:6 * gw])
    sv = _gelu(p_ref[:, 6 * gw:7 * gw])
    mu = jnp.mean(sv, axis=-1, keepdims=True)
    svc = sv - mu
    var = jnp.mean(svc * svc, axis=-1, keepdims=True)
    svn = (svc * lax.rsqrt(var + NORM_EPS) * lng_ref[...] + lnb_ref[...]).astype(BF16)
    r4 = lax.broadcasted_iota(jnp.int32, (t, nh * t), 0)
    c4 = lax.broadcasted_iota(jnp.int32, (t, nh * t), 1) & (t - 1)
    wm = jnp.where(r4 >= c4, ws_ref[...], 0.0).astype(BF16)
    s_gate = jnp.dot(wm, head_rows(svn), preferred_element_type=F32) + bs_ref[...]
    y_ref[:, 2 * gw:3 * gw] = (su * s_gate).astype(y_ref.dtype)

    v_all = jnp.concatenate(v_new, axis=0).astype(BF16)
    o_dn = jnp.concatenate(o_cross, axis=0) + jnp.dot(side_by_side(attn), head_rows(v_all),
                                                      preferred_element_type=F32)
    o_dn = o_dn * lax.rsqrt(head_sum(o_dn * o_dn) * (1.0 / HEAD_DIM) + NORM_EPS) * dng_ref[...]
    y_ref[:, 3 * gw:4 * gw] = (o_dn * _silu(p_ref[:, 10 * gw:11 * gw])).astype(y_ref.dtype)


def _mixers(p, batch, seq, consts, lp):
    t = MIX_TILE
    nt = seq // t
    gw = GROUP_WIDTH
    const_inputs = [consts["dmat"], consts["kw"], consts["qw"], consts["gc"], consts["expand"]]
    layer_inputs = [lp["pool_w"], lp["pool_scale"], lp["ln_g"], lp["ln_b"], lp["ws"], lp["bs"],
                    lp["conv_w"], lp["a_log"], lp["dt_bias"], lp["dn_g"]]
    in_specs = ([pl.BlockSpec((t, P_PAD), lambda b, i: (b * nt + i, 0)),
                 pl.BlockSpec((t, LANE), lambda b, i: (i, 0)),
                 pl.BlockSpec((t, LANE), lambda b, i: (i, 0))]
                + [_resident(a.shape) for a in const_inputs + layer_inputs])
    return pl.pallas_call(
        _mix_kernel,
        grid=(batch, nt),
        in_specs=in_specs,
        out_specs=pl.BlockSpec((t, 4 * gw), lambda b, i: (b * nt + i, 0)),
        out_shape=jax.ShapeDtypeStruct((batch * seq, 4 * gw), BF16),
        scratch_shapes=[pltpu.VMEM((POOL_HALO + t, gw), F32),
                        pltpu.VMEM((POOL_HALO + t, gw), F32),
                        pltpu.VMEM((POOL_HALO + t, gw), F32),
                        pltpu.VMEM((POOL_HALO + t, LANE), F32),
                        pltpu.VMEM((CONV_HALO + t, 3 * gw), F32),
                        pltpu.VMEM((gw, gw), F32),
                        pltpu.VMEM((gw, gw), F32)],
        compiler_params=pltpu.CompilerParams(dimension_semantics=("arbitrary", "arbitrary"),
                                             vmem_limit_bytes=VMEM_LIMIT),
        name="mixers",
    )(p, consts["cos"], consts["sin"], *const_inputs, *layer_inputs)


def _out_proj_kernel(y_ref, x_ref, w_ref, g_ref, o_ref):
    z = jnp.dot(y_ref[...], w_ref[...], preferred_element_type=F32)
    o_ref[...] = x_ref[...] + _rms_rows(z) * g_ref[...]


def _out_proj(y, x2d, w_bf16, g):
    rows, d = x2d.shape
    tm = min(ROW_TILE, rows)
    return pl.pallas_call(
        _out_proj_kernel,
        grid=(rows // tm,),
        in_specs=[pl.BlockSpec((tm, y.shape[1]), lambda i: (i, 0)),
                  pl.BlockSpec((tm, d), lambda i: (i, 0)),
                  _resident(w_bf16.shape),
                  _resident((1, d))],
        out_specs=pl.BlockSpec((tm, d), lambda i: (i, 0)),
        out_shape=jax.ShapeDtypeStruct((rows, d), F32),
        compiler_params=pltpu.CompilerParams(dimension_semantics=("arbitrary",),
                                             vmem_limit_bytes=VMEM_LIMIT),
        name="out_proj",
    )(y, x2d, w_bf16, g)


def _ffn_kernel(x_ref, gpre_ref, wup_ref, cw_ref, cb_ref, wdn_ref, gpost_ref, o_ref,
                acc_ref, ext_ref, carry_ref, *, tiles_per_seq):
    tm = x_ref.shape[0]
    d_ff = wdn_ref.shape[0]
    tf = FF_TILE
    nf = d_ff // tf

    @pl.when((pl.program_id(0) % tiles_per_seq) == 0)
    def _():
        carry_ref[...] = jnp.zeros_like(carry_ref)

    h = (_rms_rows(x_ref[...]) * gpre_ref[...]).astype(BF16)

    def up(j):
        a = jnp.dot(h, wup_ref[:, j * tf:(j + 1) * tf], preferred_element_type=F32)
        b = jnp.dot(h, wup_ref[:, d_ff + j * tf:d_ff + (j + 1) * tf], preferred_element_type=F32)
        return a, b

    def gate(j, a, b):
        cols = slice(j * tf, (j + 1) * tf)
        slot = j % 2
        ext_ref[slot, 0:CONV_HALO, :] = carry_ref[:, cols]
        ext_ref[slot, CONV_HALO:CONV_HALO + tm, :] = a
        carry_ref[:, cols] = a[tm - CONV_HALO:, :]
        conv = a * cw_ref[FFN_CONV - 1:FFN_CONV, cols] + cb_ref[:, cols]
        for k in range(1, FFN_CONV):
            conv = conv + (ext_ref[slot, pl.ds(CONV_HALO - k, tm), :]
                           * cw_ref[FFN_CONV - 1 - k:FFN_CONV - k, cols])
        return (_gelu(conv) * b).astype(BF16)

    ab = up(0)
    for j in range(nf):
        nxt = up(j + 1) if j + 1 < nf else None
        contrib = jnp.dot(gate(j, *ab), wdn_ref[j * tf:(j + 1) * tf, :], preferred_element_type=F32)
        if j == 0:
            acc_ref[...] = contrib
        else:
            acc_ref[...] += contrib
        ab = nxt
    o_ref[...] = x_ref[...] + _rms_rows(acc_ref[...]) * gpost_ref[...]


def _ffn(x2d, seq, g_pre, w_up, conv_w, conv_b, w_down, g_post):
    rows, d = x2d.shape
    d_ff = w_down.shape[0]
    tm = min(ROW_TILE, rows)
    return pl.pallas_call(
        functools.partial(_ffn_kernel, tiles_per_seq=seq // tm),
        grid=(rows // tm,),
        in_specs=[pl.BlockSpec((tm, d), lambda i: (i, 0)),
                  _resident((1, d)),
                  _resident(w_up.shape),
                  _resident(conv_w.shape),
                  _resident(conv_b.shape),
                  _resident(w_down.shape),
                  _resident((1, d))],
        out_specs=pl.BlockSpec((tm, d), lambda i: (i, 0)),
        out_shape=jax.ShapeDtypeStruct((rows, d), F32),
        scratch_shapes=[pltpu.VMEM((tm, d), F32),
                        pltpu.VMEM((2, CONV_HALO + tm, FF_TILE), F32),
                        pltpu.VMEM((CONV_HALO, d_ff), F32)],
        compiler_params=pltpu.CompilerParams(dimension_semantics=("arbitrary",),
                                             vmem_limit_bytes=VMEM_LIMIT),
        name="conv_ffn",
    )(x2d, g_pre, w_up, conv_w, conv_b, w_down, g_post)


def _shape_constants(seq):
    half = HEAD_DIM // 2
    inv = 1.0 / (ROPE_BASE ** (jnp.arange(0, HEAD_DIM, 2, dtype=F32) / HEAD_DIM))
    ang = jnp.arange(seq, dtype=F32)[:, None] * inv[None, :]
    cos, sin = jnp.cos(ang), jnp.sin(ang)
    cos_l = jnp.tile(cos, (1, LANE // half))
    sin_l = jnp.tile(jnp.concatenate([-sin, sin], axis=1), (1, LANE // HEAD_DIM))
    c = RET_CHUNK
    log_gamma = jnp.log(1.0 - 2.0 ** (-5.0 - jnp.arange(N_HEADS, dtype=F32)))
    pos = jnp.arange(c, dtype=F32)
    diff = pos[:, None] - pos[None, :]
    dmat = jnp.where(diff >= 0, jnp.exp(log_gamma[:, None, None] * jnp.maximum(diff, 0.0)), 0.0)
    k_w = jnp.exp(log_gamma[None, :] * (c - 1.0 - pos)[:, None])
    q_w = jnp.exp(log_gamma[None, :] * (pos + 1.0)[:, None])
    g_chunk = jnp.exp(log_gamma * c)[None, :]
    rep = lambda m: jnp.repeat(m, HEAD_DIM, axis=1)
    src = jnp.arange(LANE)[:, None]
    dst = jnp.arange(2 * GROUP_WIDTH)[None, :]
    expand = (src == dst // HEAD_DIM).astype(BF16)
    return {"cos": cos_l, "sin": sin_l, "dmat": dmat.reshape(N_HEADS * c, c), "kw": rep(k_w),
            "qw": rep(q_w), "gc": rep(g_chunk), "expand": expand}


def _layer_params(l, w_in, pool_w, pool_scale, sgu_ln_g, sgu_ln_b, sgu_ws, sgu_bs, dn_conv_w,
                  dn_a_log, dn_dt_bias, dn_norm_g):
    w_in_p = jnp.pad(w_in[l], ((0, 0), (0, P_PAD - w_in.shape[2]))).astype(BF16)
    pool_bd = jnp.zeros((GROUP_WIDTH, GROUP_WIDTH), F32)
    for g in range(N_HEADS):
        pool_bd = pool_bd.at[g * HEAD_DIM:(g + 1) * HEAD_DIM, g * HEAD_DIM:(g + 1) * HEAD_DIM].set(pool_w[l, g])
    gate_pad = jnp.zeros((1, LANE), F32)
    return {
        "w_in": w_in_p,
        "pool_w": pool_bd.astype(BF16),
        "pool_scale": pool_scale[l][None, :],
        "ln_g": sgu_ln_g[l][None, :],
        "ln_b": sgu_ln_b[l][None, :],
        "ws": sgu_ws[l].transpose(1, 0, 2).reshape(SGU_CHUNK, N_HEADS * SGU_CHUNK),
        "bs": jnp.repeat(sgu_bs[l].T, HEAD_DIM, axis=1),
        "conv_w": dn_conv_w[l],
        "a_log": gate_pad.at[0, N_HEADS:2 * N_HEADS].set(dn_a_log[l]),
        "dt_bias": gate_pad.at[0, N_HEADS:2 * N_HEADS].set(dn_dt_bias[l]),
        "dn_g": jnp.tile(dn_norm_g[l], N_HEADS)[None, :],
    }


def kernel(x, norm_pre_mix, norm_post_mix, norm_pre_ffn, norm_post_ffn, w_in, pool_w, pool_scale, sgu_ln_g, sgu_ln_b, sgu_ws, sgu_bs, dn_conv_w, dn_a_log, dn_dt_bias, dn_norm_g, w_out, ffn_w_up, ffn_conv_w, ffn_conv_b, ffn_w_down):
    batch, seq, d = x.shape
    depth = w_in.shape[0]
    consts = _shape_constants(seq)
    x2d = x.reshape(batch * seq, d)
    for l in range(depth):
        lp = _layer_params(l, w_in, pool_w, pool_scale, sgu_ln_g, sgu_ln_b, sgu_ws, sgu_bs,
                           dn_conv_w, dn_a_log, dn_dt_bias, dn_norm_g)
        p = _in_proj(x2d, norm_pre_mix[l][None, :], lp["w_in"])
        y = _mixers(p, batch, seq, consts, lp)
        x2d = _out_proj(y, x2d, w_out[l].astype(BF16), norm_post_mix[l][None, :])
        x2d = _ffn(x2d, seq, norm_pre_ffn[l][None, :], ffn_w_up[l].astype(BF16), ffn_conv_w[l],
                   ffn_conv_b[l][None, :], ffn_w_down[l].astype(BF16), norm_post_ffn[l][None, :])
    return x2d.reshape(batch, seq, d)
```

```python
import functools
import itertools

import jax
import jax.numpy as jnp
from jax import lax
from jax.experimental import pallas as pl
from jax.experimental.pallas import tpu as pltpu

F32 = jnp.float32
BF16 = jnp.bfloat16

NORM_EPS = 1e-6
GROUP_WIDTH = 256
HEAD_DIM = 64
N_HEADS = 4
POOL_WINDOWS = (2, 4, 8, 16)
POOL_HALO = 32
RET_CHUNK = 128
SGU_CHUNK = 128
DN_CHUNK = 64
DN_CONV = 4
CONV_HALO = 8
ROPE_BASE = 10000.0
FFN_CONV = 3
LANE = 128
P_MAIN = 11 * GROUP_WIDTH
P_PAD = P_MAIN + LANE

MIX_TILE = 128
MIX_STREAMS = 2
ROW_TILE = 512
FF_TILE = 256
VMEM_LIMIT = 48 * 1024 * 1024


def _bdot(a, b):
    return jnp.dot(a.astype(BF16), b.astype(BF16), preferred_element_type=F32)


def _bdot_nt(a, b):
    return lax.dot_general(a.astype(BF16), b.astype(BF16), (((1,), (1,)), ((), ())),
                           preferred_element_type=F32)


def _bdot_tn(a, b):
    return lax.dot_general(a.astype(BF16), b.astype(BF16), (((0,), (0,)), ((), ())),
                           preferred_element_type=F32)


def _split(x, terms):
    parts = []
    r = x
    for i in range(terms):
        part = r.astype(BF16)
        parts.append(part)
        if i + 1 < terms:
            r = r - part.astype(F32)
    return parts


def _split_dot(a_bf16, x, terms):
    return sum(jnp.dot(a_bf16, part, preferred_element_type=F32) for part in _split(x, terms))


def _split_dot_rhs(x, b_bf16, terms):
    return sum(jnp.dot(part, b_bf16, preferred_element_type=F32) for part in _split(x, terms))


def _rms_rows(x):
    return x * lax.rsqrt(jnp.mean(x * x, axis=-1, keepdims=True) + NORM_EPS)


def _softplus(x):
    return jnp.maximum(x, 0.0) + jnp.log1p(jnp.exp(-jnp.abs(x)))


def _gelu(x):
    return jax.nn.gelu(x, approximate=True)


def _silu(x):
    return x * jax.nn.sigmoid(x)


def _resident(shape):
    nd = len(shape)
    return pl.BlockSpec(shape, lambda *_: (0,) * nd, pipeline_mode=pl.Buffered(1))


def _in_proj_kernel(x_ref, g_ref, w_ref, p_ref):
    h = (_rms_rows(x_ref[...]) * g_ref[...]).astype(BF16)
    n = p_ref.shape[1]
    for c0 in range(0, n, 512):
        c1 = min(c0 + 512, n)
        p_ref[:, c0:c1] = jnp.dot(h, w_ref[:, c0:c1], preferred_element_type=F32)


def _in_proj(x2d, g, w_bf16):
    rows, d = x2d.shape
    n = w_bf16.shape[1]
    tm = min(ROW_TILE, rows)
    return pl.pallas_call(
        _in_proj_kernel,
        grid=(rows // tm,),
        in_specs=[pl.BlockSpec((tm, d), lambda i: (i, 0)),
                  _resident((1, d)),
                  _resident((d, n))],
        out_specs=pl.BlockSpec((tm, n), lambda i: (i, 0)),
        out_shape=jax.ShapeDtypeStruct((rows, n), F32),
        compiler_params=pltpu.CompilerParams(dimension_semantics=("arbitrary",),
                                             vmem_limit_bytes=VMEM_LIMIT),
        name="in_proj",
    )(x2d, g, w_bf16)


def _mix_kernel(p_all, cos_ref, sin_ref, dmat_ref, kw_ref, qw_ref, gc_ref, expand_ref,
                poolw_ref, pools_ref, lng_ref, lnb_ref, ws_ref, bs_ref,
                convw_ref, alog_ref, dtb_ref, dng_ref,
                y_all,
                pool_ext_all, pool_s2_all, pool_s4_all, pool_s8_all, conv_ext_all, s_ret_all, s_dn_all):
    t = MIX_TILE
    gw = GROUP_WIDTH
    nh = N_HEADS

    @pl.when(pl.program_id(1) == 0)
    def _():
        pool_ext_all[:, 0:POOL_HALO, :] = jnp.zeros((MIX_STREAMS, POOL_HALO, gw), F32)
        conv_ext_all[:, 0:CONV_HALO, :] = jnp.zeros((MIX_STREAMS, CONV_HALO, 3 * gw), F32)
        s_ret_all[...] = jnp.zeros_like(s_ret_all)
        s_dn_all[...] = jnp.zeros_like(s_dn_all)

    lane = lax.broadcasted_iota(jnp.int32, (t, gw), 1)
    head_of_lane = lane >> 6
    mb = [jnp.where(head_of_lane == h, 1.0, 0.0).astype(BF16) for h in range(nh)]
    mb2 = [jnp.concatenate([m, m], axis=1) for m in mb]
    row_t = lax.broadcasted_iota(jnp.int32, (t, gw), 0)
    lane128 = lax.broadcasted_iota(jnp.int32, (t, LANE), 1)
    sr = lax.broadcasted_iota(jnp.int32, (gw, gw), 0) >> 6
    sc = lax.broadcasted_iota(jnp.int32, (gw, gw), 1) >> 6
    state_mask = sr == sc
    head_ones = jnp.where(state_mask, 1.0, 0.0).astype(BF16)
    rr = lax.broadcasted_iota(jnp.int32, (t, t), 0)
    cc = lax.broadcasted_iota(jnp.int32, (t, t), 1)
    same_chunk = (rr >> 6) == (cc >> 6)
    incl = same_chunk & (rr >= cc)
    strict = same_chunk & (rr > cc)
    tril_bd = jnp.where(incl, 1.0, 0.0).astype(BF16)
    eye = jnp.where(rr == cc, 1.0, 0.0)
    r4 = lax.broadcasted_iota(jnp.int32, (t, nh * t), 0)
    c4 = lax.broadcasted_iota(jnp.int32, (t, nh * t), 1) & (t - 1)
    wm = jnp.where(r4 >= c4, ws_ref[...], 0.0).astype(BF16)
    first_half = (lane128 & (HEAD_DIM - 1)) < (HEAD_DIM // 2)
    low = lane128 < HEAD_DIM
    win = jnp.concatenate([jnp.where(low, float(POOL_WINDOWS[0]), float(POOL_WINDOWS[1])),
                           jnp.where(low, float(POOL_WINDOWS[2]), float(POOL_WINDOWS[3]))], axis=1)
    pos = (pl.program_id(1) * t + row_t + 1).astype(F32)
    inv_count = 1.0 / jnp.minimum(pos, win)
    n_chunks = t // DN_CHUNK

    def head_sum(x):
        return _split_dot_rhs(x, head_ones, 2)

    def head_rows(x_bf16):
        masks = mb if x_bf16.shape[1] == gw else mb2
        return jnp.concatenate([x_bf16 * masks[h] for h in range(nh)], axis=0)

    def side_by_side(blocks):
        return jnp.concatenate([b.astype(BF16) for b in blocks], axis=1)

    def stream(s):
        p_ref = p_all.at[s]
        y_ref = y_all.at[s]
        pool_ext, pool_s2, pool_s4, pool_s8 = (r.at[s] for r in (pool_ext_all, pool_s2_all, pool_s4_all, pool_s8_all))
        conv_ext, s_ret, s_dn = conv_ext_all.at[s], s_ret_all.at[s], s_dn_all.at[s]

        conv_ext[CONV_HALO:CONV_HALO + t, :] = p_ref[:, 7 * gw:10 * gw]
        qkv = None
        for j in range(DN_CONV):
            term = conv_ext[pl.ds(CONV_HALO - (DN_CONV - 1) + j, t), :] * convw_ref[j:j + 1, :]
            qkv = term if qkv is None else qkv + term
        conv_ext[0:CONV_HALO, :] = conv_ext[t:t + CONV_HALO, :]
        qkv = _silu(qkv)
        dq = qkv[:, 0:gw]
        dk = qkv[:, gw:2 * gw]
        dv = qkv[:, 2 * gw:3 * gw]
        yield
        dq = dq * lax.rsqrt(head_sum(dq * dq) + NORM_EPS) * (HEAD_DIM ** -0.5)
        dk = dk * lax.rsqrt(head_sum(dk * dk) + NORM_EPS)
        yield

        ba = p_ref[:, P_MAIN:P_PAD]
        g_c = -jnp.exp(alog_ref[...]) * _softplus(ba + dtb_ref[...])
        decay_c = _split_dot(tril_bd, g_c, 3)
        gates = jnp.where(lane128 < nh, jax.nn.sigmoid(ba), decay_c)
        gates = _split_dot_rhs(gates, expand_ref[...], 3)
        beta = gates[:, 0:gw]
        decay = gates[:, gw:2 * gw]
        decay_rows = decay_c.T
        yield

        exp_decay = jnp.exp(decay)
        k_beta = dk * beta
        rhs = jnp.concatenate([dv * beta, k_beta * exp_decay], axis=1).astype(BF16)
        q_dec = dq * exp_decay
        d_last = [decay[(c + 1) * DN_CHUNK - 1:(c + 1) * DN_CHUNK, :] for c in range(n_chunks)]
        d_last_rows = jnp.concatenate([jnp.broadcast_to(dl, (DN_CHUNK, gw)) for dl in d_last], axis=0)
        k_tail = dk * jnp.exp(d_last_rows - decay)
        dkb = dk.astype(BF16)
        kq = _bdot_nt(jnp.concatenate([head_rows(k_beta.astype(BF16)), head_rows(dq.astype(BF16))], axis=0),
                      dkb)
        yield
        power = []
        attn = []
        for h in range(nh):
            col = jnp.broadcast_to(decay_c[:, nh + h:nh + h + 1], (t, t))
            row = jnp.broadcast_to(decay_rows[nh + h:nh + h + 1, :], (t, t))
            dmask = jnp.where(incl, jnp.exp(col - row), 0.0)
            power.append(jnp.where(strict, kq[h * t:(h + 1) * t] * dmask, 0.0))
            attn.append(kq[(nh + h) * t:(nh + h + 1) * t] * dmask)
        yield
        inv = [eye - lmat for lmat in power]
        for _ in range(5):
            power = [_bdot(pw, pw) for pw in power]
            inv = [iv + _bdot(iv, pw) for iv, pw in zip(inv, power)]
            yield
        sol = jnp.dot(side_by_side(inv), head_rows(rhs), preferred_element_type=F32)
        u = sol[:, 0:gw]
        w = sol[:, gw:2 * gw]
        yield

        a = p_ref[:, 0:gw]
        n = POOL_HALO + t
        pool_ext[POOL_HALO:n, :] = a
        s2 = pool_ext[8:n, :] + pool_ext[pl.ds(7, n - 8), :]
        pool_s2[8:n, :] = s2
        s4 = pool_s2[16:n, :] + pool_s2[pl.ds(14, n - 16), :]
        pool_s4[16:n, :] = s4
        s8 = pool_s4[24:n, LANE:] + pool_s4[pl.ds(20, n - 24), LANE:]
        pool_s8[24:n, :] = s8
        s16 = pool_s8[32:n, :] + pool_s8[24:n - 8, :]
        pool_ext[0:POOL_HALO, :] = pool_ext[t:n, :]
        win_sum = jnp.concatenate([jnp.where(low, s2[24:, 0:LANE], s4[16:, 0:LANE]),
                                   jnp.where(low, s8[8:, :], s16)], axis=1)
        d_pool = win_sum * inv_count - a
        y_a = _bdot(d_pool, poolw_ref[...]) * pools_ref[...]
        y_ref[:, 0:gw] = y_a.astype(y_ref.dtype)
        yield

        state = s_dn[...]
        v_new = []
        o_cross = []
        for c in range(n_chunks):
            r0, r1 = c * DN_CHUNK, (c + 1) * DN_CHUNK
            sb = state.astype(BF16)
            vn = u[r0:r1] - _bdot(w[r0:r1], sb)
            o_cross.append(_bdot(q_dec[r0:r1], sb))
            state = state * jnp.exp(d_last[c]) + jnp.where(state_mask, _bdot_tn(k_tail[r0:r1], vn), 0.0)
            v_new.append(vn)
            yield
        s_dn[...] = state

        cos = cos_ref[...]
        sin = sin_ref[...]

        def rope(xh):
            partner = jnp.where(first_half, pltpu.roll(xh, LANE - HEAD_DIM // 2, 1),
                                pltpu.roll(xh, HEAD_DIM // 2, 1))
            return xh * cos + partner * sin

        def rope2(x):
            return jnp.concatenate([rope(x[:, :LANE]), rope(x[:, LANE:])], axis=1)

        rq = rope2(p_ref[:, gw:2 * gw])
        rk = rope2(p_ref[:, 2 * gw:3 * gw]) * (HEAD_DIM ** -0.5)
        rkb = rk.astype(BF16)
        rvb = p_ref[:, 3 * gw:4 * gw].astype(BF16)
        yield
        scores = _bdot_nt(head_rows(rq.astype(BF16)), rkb) * dmat_ref[...]
        o_ret = (_bdot(rq * qw_ref[...], s_ret[...])
                 + jnp.dot(side_by_side([scores[h * t:(h + 1) * t] for h in range(nh)]), head_rows(rvb),
                           preferred_element_type=F32))
        kv = _bdot_tn(rk * kw_ref[...], rvb)
        s_ret[...] = s_ret[...] * gc_ref[...] + jnp.where(state_mask, kv, 0.0)
        yield
        o_ret = o_ret * lax.rsqrt(head_sum(o_ret * o_ret) * (1.0 / HEAD_DIM) + NORM_EPS)
        y_ref[:, gw:2 * gw] = (o_ret * _silu(p_ref[:, 4 * gw:5 * gw])).astype(y_ref.dtype)
        yield

        su = _gelu(p_ref[:, 5 * gw:6 * gw])
        sv = _gelu(p_ref[:, 6 * gw:7 * gw])
        mu = jnp.mean(sv, axis=-1, keepdims=True)
        svc = sv - mu
        var = jnp.mean(svc * svc, axis=-1, keepdims=True)
        svn = (svc * lax.rsqrt(var + NORM_EPS) * lng_ref[...] + lnb_ref[...]).astype(BF16)
        s_gate = jnp.dot(wm, head_rows(svn), preferred_element_type=F32) + bs_ref[...]
        y_ref[:, 2 * gw:3 * gw] = (su * s_gate).astype(y_ref.dtype)
        yield

        v_all = jnp.concatenate(v_new, axis=0).astype(BF16)
        o_dn = jnp.concatenate(o_cross, axis=0) + jnp.dot(side_by_side(attn), head_rows(v_all),
                                                          preferred_element_type=F32)
        o_dn = o_dn * lax.rsqrt(head_sum(o_dn * o_dn) * (1.0 / HEAD_DIM) + NORM_EPS) * dng_ref[...]
        y_ref[:, 3 * gw:4 * gw] = (o_dn * _silu(p_ref[:, 10 * gw:11 * gw])).astype(y_ref.dtype)

    for _ in itertools.zip_longest(*[stream(s) for s in range(MIX_STREAMS)]):
        pass


def _mixers(p, batch, seq, consts, lp):
    t = MIX_TILE
    ns = MIX_STREAMS
    nt = seq // t
    gw = GROUP_WIDTH
    const_inputs = [consts["dmat"], consts["kw"], consts["qw"], consts["gc"], consts["expand"]]
    layer_inputs = [lp["pool_w"], lp["pool_scale"], lp["ln_g"], lp["ln_b"], lp["ws"], lp["bs"],
                    lp["conv_w"], lp["a_log"], lp["dt_bias"], lp["dn_g"]]
    in_specs = ([pl.BlockSpec((ns, t, P_PAD), lambda b, i: (b, i, 0)),
                 pl.BlockSpec((t, LANE), lambda b, i: (i, 0)),
                 pl.BlockSpec((t, LANE), lambda b, i: (i, 0))]
                + [_resident(a.shape) for a in const_inputs + layer_inputs])
    y = pl.pallas_call(
        _mix_kernel,
        grid=(batch // ns, nt),
        in_specs=in_specs,
        out_specs=pl.BlockSpec((ns, t, 4 * gw), lambda b, i: (b, i, 0)),
        out_shape=jax.ShapeDtypeStruct((batch, seq, 4 * gw), BF16),
        scratch_shapes=[pltpu.VMEM((ns, POOL_HALO + t, gw), F32),
                        pltpu.VMEM((ns, POOL_HALO + t, gw), F32),
                        pltpu.VMEM((ns, POOL_HALO + t, gw), F32),
                        pltpu.VMEM((ns, POOL_HALO + t, LANE), F32),
                        pltpu.VMEM((ns, CONV_HALO + t, 3 * gw), F32),
                        pltpu.VMEM((ns, gw, gw), F32),
                        pltpu.VMEM((ns, gw, gw), F32)],
        compiler_params=pltpu.CompilerParams(dimension_semantics=("arbitrary", "arbitrary"),
                                             vmem_limit_bytes=VMEM_LIMIT),
        name="mixers",
    )(p.reshape(batch, seq, P_PAD), consts["cos"], consts["sin"], *const_inputs, *layer_inputs)
    return y.reshape(batch * seq, 4 * gw)


def _out_proj_kernel(y_ref, x_ref, w_ref, g_ref, o_ref):
    z = jnp.dot(y_ref[...], w_ref[...], preferred_element_type=F32)
    o_ref[...] = x_ref[...] + _rms_rows(z) * g_ref[...]


def _out_proj(y, x2d, w_bf16, g):
    rows, d = x2d.shape
    tm = min(ROW_TILE, rows)
    return pl.pallas_call(
        _out_proj_kernel,
        grid=(rows // tm,),
        in_specs=[pl.BlockSpec((tm, y.shape[1]), lambda i: (i, 0)),
                  pl.BlockSpec((tm, d), lambda i: (i, 0)),
                  _resident(w_bf16.shape),
                  _resident((1, d))],
        out_specs=pl.BlockSpec((tm, d), lambda i: (i, 0)),
        out_shape=jax.ShapeDtypeStruct((rows, d), F32),
        compiler_params=pltpu.CompilerParams(dimension_semantics=("arbitrary",),
                                             vmem_limit_bytes=VMEM_LIMIT),
        name="out_proj",
    )(y, x2d, w_bf16, g)


def _ffn_kernel(x_ref, gpre_ref, wup_ref, cw_ref, cb_ref, wdn_ref, gpost_ref, o_ref,
                gated_ref, ext_ref, carry_ref, *, tiles_per_seq):
    tm = x_ref.shape[0]
    d_ff = wdn_ref.shape[0]
    tf = FF_TILE
    nf = d_ff // tf

    @pl.when((pl.program_id(0) % tiles_per_seq) == 0)
    def _():
        carry_ref[...] = jnp.zeros_like(carry_ref)

    h = (_rms_rows(x_ref[...]) * gpre_ref[...]).astype(BF16)

    def up(j):
        a = jnp.dot(h, wup_ref[:, j * tf:(j + 1) * tf], preferred_element_type=F32)
        b = jnp.dot(h, wup_ref[:, d_ff + j * tf:d_ff + (j + 1) * tf], preferred_element_type=F32)
        return a, b

    def gate(j, a, b):
        cols = slice(j * tf, (j + 1) * tf)
        slot = j % 2
        ext_ref[slot, 0:CONV_HALO, :] = carry_ref[:, cols]
        ext_ref[slot, CONV_HALO:CONV_HALO + tm, :] = a
        carry_ref[:, cols] = a[tm - CONV_HALO:, :]
        conv = a * cw_ref[FFN_CONV - 1:FFN_CONV, cols] + cb_ref[:, cols]
        for k in range(1, FFN_CONV):
            conv = conv + (ext_ref[slot, pl.ds(CONV_HALO - k, tm), :]
                           * cw_ref[FFN_CONV - 1 - k:FFN_CONV - k, cols])
        return (_gelu(conv) * b).astype(BF16)

    ab = up(0)
    for j in range(nf):
        nxt = up(j + 1) if j + 1 < nf else None
        gated_ref[:, j * tf:(j + 1) * tf] = gate(j, *ab)
        ab = nxt
    z = jnp.dot(gated_ref[...], wdn_ref[...], preferred_element_type=F32)
    o_ref[...] = x_ref[...] + _rms_rows(z) * gpost_ref[...]


def _ffn(x2d, seq, g_pre, w_up, conv_w, conv_b, w_down, g_post):
    rows, d = x2d.shape
    d_ff = w_down.shape[0]
    tm = min(ROW_TILE, rows)
    return pl.pallas_call(
        functools.partial(_ffn_kernel, tiles_per_seq=seq // tm),
        grid=(rows // tm,),
        in_specs=[pl.BlockSpec((tm, d), lambda i: (i, 0)),
                  _resident((1, d)),
                  _resident(w_up.shape),
                  _resident(conv_w.shape),
                  _resident(conv_b.shape),
                  _resident(w_down.shape),
                  _resident((1, d))],
        out_specs=pl.BlockSpec((tm, d), lambda i: (i, 0)),
        out_shape=jax.ShapeDtypeStruct((rows, d), F32),
        scratch_shapes=[pltpu.VMEM((tm, d_ff), BF16),
                        pltpu.VMEM((2, CONV_HALO + tm, FF_TILE), F32),
                        pltpu.VMEM((CONV_HALO, d_ff), F32)],
        compiler_params=pltpu.CompilerParams(dimension_semantics=("arbitrary",),
                                             vmem_limit_bytes=VMEM_LIMIT),
        name="conv_ffn",
    )(x2d, g_pre, w_up, conv_w, conv_b, w_down, g_post)


def _shape_constants(seq):
    half = HEAD_DIM // 2
    inv = 1.0 / (ROPE_BASE ** (jnp.arange(0, HEAD_DIM, 2, dtype=F32) / HEAD_DIM))
    ang = jnp.arange(seq, dtype=F32)[:, None] * inv[None, :]
    cos, sin = jnp.cos(ang), jnp.sin(ang)
    cos_l = jnp.tile(cos, (1, LANE // half))
    sin_l = jnp.tile(jnp.concatenate([-sin, sin], axis=1), (1, LANE // HEAD_DIM))
    c = RET_CHUNK
    log_gamma = jnp.log(1.0 - 2.0 ** (-5.0 - jnp.arange(N_HEADS, dtype=F32)))
    pos = jnp.arange(c, dtype=F32)
    diff = pos[:, None] - pos[None, :]
    dmat = jnp.where(diff >= 0, jnp.exp(log_gamma[:, None, None] * jnp.maximum(diff, 0.0)), 0.0)
    k_w = jnp.exp(log_gamma[None, :] * (c - 1.0 - pos)[:, None])
    q_w = jnp.exp(log_gamma[None, :] * (pos + 1.0)[:, None])
    g_chunk = jnp.exp(log_gamma * c)[None, :]
    rep = lambda m: jnp.repeat(m, HEAD_DIM, axis=1)
    src = jnp.arange(LANE)[:, None]
    dst = jnp.arange(2 * GROUP_WIDTH)[None, :]
    expand = (src == dst // HEAD_DIM).astype(BF16)
    return {"cos": cos_l, "sin": sin_l, "dmat": dmat.reshape(N_HEADS * c, c), "kw": rep(k_w),
            "qw": rep(q_w), "gc": rep(g_chunk), "expand": expand}


def _layer_params(l, w_in, pool_w, pool_scale, sgu_ln_g, sgu_ln_b, sgu_ws, sgu_bs, dn_conv_w,
                  dn_a_log, dn_dt_bias, dn_norm_g):
    w_in_p = jnp.pad(w_in[l], ((0, 0), (0, P_PAD - w_in.shape[2]))).astype(BF16)
    pool_bd = jnp.zeros((GROUP_WIDTH, GROUP_WIDTH), F32)
    for g in range(N_HEADS):
        pool_bd = pool_bd.at[g * HEAD_DIM:(g + 1) * HEAD_DIM, g * HEAD_DIM:(g + 1) * HEAD_DIM].set(pool_w[l, g])
    gate_pad = jnp.zeros((1, LANE), F32)
    return {
        "w_in": w_in_p,
        "pool_w": pool_bd.astype(BF16),
        "pool_scale": pool_scale[l][None, :],
        "ln_g": sgu_ln_g[l][None, :],
        "ln_b": sgu_ln_b[l][None, :],
        "ws": sgu_ws[l].transpose(1, 0, 2).reshape(SGU_CHUNK, N_HEADS * SGU_CHUNK),
        "bs": jnp.repeat(sgu_bs[l].T, HEAD_DIM, axis=1),
        "conv_w": dn_conv_w[l],
        "a_log": gate_pad.at[0, N_HEADS:2 * N_HEADS].set(dn_a_log[l]),
        "dt_bias": gate_pad.at[0, N_HEADS:2 * N_HEADS].set(dn_dt_bias[l]),
        "dn_g": jnp.tile(dn_norm_g[l], N_HEADS)[None, :],
    }


def kernel(x, norm_pre_mix, norm_post_mix, norm_pre_ffn, norm_post_ffn, w_in, pool_w, pool_scale, sgu_ln_g, sgu_ln_b, sgu_ws, sgu_bs, dn_conv_w, dn_a_log, dn_dt_bias, dn_norm_g, w_out, ffn_w_up, ffn_conv_w, ffn_conv_b, ffn_w_down):
    batch, seq, d = x.shape
    depth = w_in.shape[0]
    consts = _shape_constants(seq)
    x2d = x.reshape(batch * seq, d)
    for l in range(depth):
        lp = _layer_params(l, w_in, pool_w, pool_scale, sgu_ln_g, sgu_ln_b, sgu_ws, sgu_bs,
                           dn_conv_w, dn_a_log, dn_dt_bias, dn_norm_g)
        p = _in_proj(x2d, norm_pre_mix[l][None, :], lp["w_in"])
        y = _mixers(p, batch, seq, consts, lp)
        x2d = _out_proj(y, x2d, w_out[l].astype(BF16), norm_post_mix[l][None, :])
        x2d = _ffn(x2d, seq, norm_pre_ffn[l][None, :], ffn_w_up[l].astype(BF16), ffn_conv_w[l],
                   ffn_conv_b[l][None, :], ffn_w_down[l].astype(BF16), norm_post_ffn[l][None, :])
    return x2d.reshape(batch, seq, d)
```

```python
import functools
import itertools
import math

import jax
import jax.numpy as jnp
from jax import lax
from jax.experimental import pallas as pl
from jax.experimental.pallas import tpu as pltpu

F32 = jnp.float32
BF16 = jnp.bfloat16

NORM_EPS = 1e-6
GELU_C0 = math.sqrt(2.0 / math.pi)
GELU_C1 = 0.044715 * GELU_C0
GROUP_WIDTH = 256
HEAD_DIM = 64
N_HEADS = 4
POOL_WINDOWS = (2, 4, 8, 16)
POOL_HALO = 32
RET_CHUNK = 128
SGU_CHUNK = 128
DN_CHUNK = 64
DN_CONV = 4
CONV_HALO = 8
ROPE_BASE = 10000.0
FFN_CONV = 3
LANE = 128
P_MAIN = 11 * GROUP_WIDTH
P_PAD = P_MAIN + LANE

MIX_TILE = 128
MIX_STREAMS = 4
IN_PROJ_COLS = 256
ROW_TILE = 512
FF_TILE = 256
VMEM_LIMIT = 48 * 1024 * 1024


def _bdot(a, b):
    return jnp.dot(a.astype(BF16), b.astype(BF16), preferred_element_type=F32)


def _bdot_nt(a, b):
    return lax.dot_general(a.astype(BF16), b.astype(BF16), (((1,), (1,)), ((), ())),
                           preferred_element_type=F32)


def _bdot_tn(a, b):
    return lax.dot_general(a.astype(BF16), b.astype(BF16), (((0,), (0,)), ((), ())),
                           preferred_element_type=F32)


def _split(x, terms):
    parts = []
    r = x
    for i in range(terms):
        part = r.astype(BF16)
        parts.append(part)
        if i + 1 < terms:
            r = r - part.astype(F32)
    return parts


def _split_dot(a_bf16, x, terms):
    return sum(jnp.dot(a_bf16, part, preferred_element_type=F32) for part in _split(x, terms))


def _split_dot_rhs(x, b_bf16, terms):
    return sum(jnp.dot(part, b_bf16, preferred_element_type=F32) for part in _split(x, terms))


def _rms_rows(x):
    return x * lax.rsqrt(jnp.mean(x * x, axis=-1, keepdims=True) + NORM_EPS)


def _softplus(x):
    return jnp.maximum(x, 0.0) + jnp.log1p(jnp.exp(-jnp.abs(x)))


def _gelu(x):
    half = 0.5 * x
    inner = x * (GELU_C0 + GELU_C1 * (x * x))
    return half + half * jnp.tanh(inner)


def _silu(x):
    half = 0.5 * x
    return half + half * jnp.tanh(half)


def _resident(shape):
    nd = len(shape)
    return pl.BlockSpec(shape, lambda *_: (0,) * nd, pipeline_mode=pl.Buffered(1))


def _resident_layer(stacked, l):
    nd = stacked.ndim - 1
    return pl.BlockSpec((None,) + stacked.shape[1:], lambda *_: (l,) + (0,) * nd,
                        pipeline_mode=pl.Buffered(1))


def _mix_kernel(x_first, x_next, gpre_ref, win_ref,
                cos_ref, sin_ref, dmat_ref, kw_ref, qw_ref, gc_ref, expand_ref,
                poolw_ref, pools_ref, lng_ref, lnb_ref, ws_ref, bs_ref,
                convw_ref, alog_ref, dtb_ref, dng_ref,
                y_all,
                p_buf, pool_ext_all, pool_s2_all, pool_s4_all, pool_s8_all, conv_ext_all, s_ret_all, s_dn_all):
    t = MIX_TILE
    gw = GROUP_WIDTH
    nh = N_HEADS
    ns = MIX_STREAMS
    cur = pl.program_id(1) % 2
    nxt = 1 - cur

    def in_proj(x_ref, slot):
        x = x_ref[...].reshape(ns * t, x_ref.shape[2])
        h = (_rms_rows(x) * gpre_ref[...]).astype(BF16)
        yield
        for c0 in range(0, P_PAD, IN_PROJ_COLS):
            c1 = min(c0 + IN_PROJ_COLS, P_PAD)
            res = jnp.dot(h, win_ref[:, c0:c1], preferred_element_type=F32)
            for s in range(ns):
                p_buf[slot, s, :, c0:c1] = res[s * t:(s + 1) * t]
            yield

    @pl.when(pl.program_id(1) == 0)
    def _():
        pool_ext_all[:, 0:POOL_HALO, :] = jnp.zeros((MIX_STREAMS, POOL_HALO, gw), F32)
        conv_ext_all[:, 0:CONV_HALO, :] = jnp.zeros((MIX_STREAMS, CONV_HALO, 3 * gw), F32)
        s_ret_all[...] = jnp.zeros_like(s_ret_all)
        s_dn_all[...] = jnp.zeros_like(s_dn_all)
        for _ in in_proj(x_first, 0):
            pass

    lane = lax.broadcasted_iota(jnp.int32, (t, gw), 1)
    head_of_lane = lane >> 6
    mb = [jnp.where(head_of_lane == h, 1.0, 0.0).astype(BF16) for h in range(nh)]
    mb2 = [jnp.concatenate([m, m], axis=1) for m in mb]
    row_t = lax.broadcasted_iota(jnp.int32, (t, gw), 0)
    lane128 = lax.broadcasted_iota(jnp.int32, (t, LANE), 1)
    sr = lax.broadcasted_iota(jnp.int32, (gw, gw), 0) >> 6
    sc = lax.broadcasted_iota(jnp.int32, (gw, gw), 1) >> 6
    state_mask = sr == sc
    head_ones = jnp.where(state_mask, 1.0, 0.0).astype(BF16)
    rr = lax.broadcasted_iota(jnp.int32, (t, t), 0)
    cc = lax.broadcasted_iota(jnp.int32, (t, t), 1)
    same_chunk = (rr >> 6) == (cc >> 6)
    incl = same_chunk & (rr >= cc)
    strict = same_chunk & (rr > cc)
    tril_bd = jnp.where(incl, 1.0, 0.0).astype(BF16)
    eye = jnp.where(rr == cc, 1.0, 0.0)
    r4 = lax.broadcasted_iota(jnp.int32, (t, nh * t), 0)
    c4 = lax.broadcasted_iota(jnp.int32, (t, nh * t), 1) & (t - 1)
    wm = jnp.where(r4 >= c4, ws_ref[...], 0.0).astype(BF16)
    first_half = (lane128 & (HEAD_DIM - 1)) < (HEAD_DIM // 2)
    low = lane128 < HEAD_DIM
    win = jnp.concatenate([jnp.where(low, float(POOL_WINDOWS[0]), float(POOL_WINDOWS[1])),
                           jnp.where(low, float(POOL_WINDOWS[2]), float(POOL_WINDOWS[3]))], axis=1)
    pos = (pl.program_id(1) * t + row_t + 1).astype(F32)
    inv_count = 1.0 / jnp.minimum(pos, win)
    n_chunks = t // DN_CHUNK

    def head_sum(x):
        return jnp.dot(x.astype(BF16), head_ones, preferred_element_type=F32)

    def head_rows(x_bf16):
        masks = mb if x_bf16.shape[1] == gw else mb2
        return jnp.concatenate([x_bf16 * masks[h] for h in range(nh)], axis=0)

    def side_by_side(blocks):
        return jnp.concatenate([b.astype(BF16) for b in blocks], axis=1)

    def stream(s):
        p_ref = p_buf.at[cur, s]
        y_ref = y_all.at[s]
        pool_ext, pool_s2, pool_s4, pool_s8 = (r.at[s] for r in (pool_ext_all, pool_s2_all, pool_s4_all, pool_s8_all))
        conv_ext, s_ret, s_dn = conv_ext_all.at[s], s_ret_all.at[s], s_dn_all.at[s]

        conv_ext[CONV_HALO:CONV_HALO + t, :] = p_ref[:, 7 * gw:10 * gw]
        qkv = None
        for j in range(DN_CONV):
            term = conv_ext[pl.ds(CONV_HALO - (DN_CONV - 1) + j, t), :] * convw_ref[j:j + 1, :]
            qkv = term if qkv is None else qkv + term
        conv_ext[0:CONV_HALO, :] = conv_ext[t:t + CONV_HALO, :]
        qkv = _silu(qkv)
        dq = qkv[:, 0:gw]
        dk = qkv[:, gw:2 * gw]
        dv = qkv[:, 2 * gw:3 * gw]
        yield
        dq = dq * lax.rsqrt(head_sum(dq * dq) + NORM_EPS) * (HEAD_DIM ** -0.5)
        dk = dk * lax.rsqrt(head_sum(dk * dk) + NORM_EPS)
        yield

        ba = p_ref[:, P_MAIN:P_PAD]
        g_c = -jnp.exp(alog_ref[...]) * _softplus(ba + dtb_ref[...])
        decay_c = _split_dot(tril_bd, g_c, 3)
        gates = jnp.where(lane128 < nh, jax.nn.sigmoid(ba), decay_c)
        gates = _split_dot_rhs(gates, expand_ref[...], 3)
        beta = gates[:, 0:gw]
        decay = gates[:, gw:2 * gw]
        decay_rows = decay_c.T
        yield

        exp_decay = jnp.exp(decay)
        k_beta = dk * beta
        rhs = jnp.concatenate([dv * beta, k_beta * exp_decay], axis=1).astype(BF16)
        q_dec = dq * exp_decay
        d_last = [decay[(c + 1) * DN_CHUNK - 1:(c + 1) * DN_CHUNK, :] for c in range(n_chunks)]
        d_last_rows = jnp.concatenate([jnp.broadcast_to(dl, (DN_CHUNK, gw)) for dl in d_last], axis=0)
        k_tail = dk * jnp.exp(d_last_rows - decay)
        dkb = dk.astype(BF16)
        kq = _bdot_nt(jnp.concatenate([head_rows(k_beta.astype(BF16)), head_rows(dq.astype(BF16))], axis=0),
                      dkb)
        yield
        power = []
        attn = []
        for h in range(nh):
            col = jnp.broadcast_to(decay_c[:, nh + h:nh + h + 1], (t, t))
            row = jnp.broadcast_to(decay_rows[nh + h:nh + h + 1, :], (t, t))
            dmask = jnp.where(incl, jnp.exp(col - row), 0.0)
            power.append(jnp.where(strict, kq[h * t:(h + 1) * t] * dmask, 0.0))
            attn.append(kq[(nh + h) * t:(nh + h + 1) * t] * dmask)
        yield
        inv = [eye - lmat for lmat in power]
        for _ in range(5):
            power = [_bdot(pw, pw) for pw in power]
            inv = [iv + _bdot(iv, pw) for iv, pw in zip(inv, power)]
            yield
        sol = jnp.dot(side_by_side(inv), head_rows(rhs), preferred_element_type=F32)
        u = sol[:, 0:gw]
        w = sol[:, gw:2 * gw]
        yield

        a = p_ref[:, 0:gw]
        n = POOL_HALO + t
        pool_ext[POOL_HALO:n, :] = a
        s2 = pool_ext[8:n, :] + pool_ext[pl.ds(7, n - 8), :]
        pool_s2[8:n, :] = s2
        s4 = pool_s2[16:n, :] + pool_s2[pl.ds(14, n - 16), :]
        pool_s4[16:n, :] = s4
        s8 = pool_s4[24:n, LANE:] + pool_s4[pl.ds(20, n - 24), LANE:]
        pool_s8[24:n, :] = s8
        s16 = pool_s8[32:n, :] + pool_s8[24:n - 8, :]
        pool_ext[0:POOL_HALO, :] = pool_ext[t:n, :]
        win_sum = jnp.concatenate([jnp.where(low, s2[24:, 0:LANE], s4[16:, 0:LANE]),
                                   jnp.where(low, s8[8:, :], s16)], axis=1)
        d_pool = win_sum * inv_count - a
        y_a = _bdot(d_pool, poolw_ref[...]) * pools_ref[...]
        y_ref[:, 0:gw] = y_a.astype(y_ref.dtype)
        yield

        state = s_dn[...]
        v_new = []
        o_cross = []
        for c in range(n_chunks):
            r0, r1 = c * DN_CHUNK, (c + 1) * DN_CHUNK
            sb = state.astype(BF16)
            vn = u[r0:r1] - _bdot(w[r0:r1], sb)
            o_cross.append(_bdot(q_dec[r0:r1], sb))
            state = state * jnp.exp(d_last[c]) + jnp.where(state_mask, _bdot_tn(k_tail[r0:r1], vn), 0.0)
            v_new.append(vn)
            yield
        s_dn[...] = state

        cos = cos_ref[...]
        sin = sin_ref[...]

        def rope(xh):
            partner = jnp.where(first_half, pltpu.roll(xh, LANE - HEAD_DIM // 2, 1),
                                pltpu.roll(xh, HEAD_DIM // 2, 1))
            return xh * cos + partner * sin

        def rope2(x):
            return jnp.concatenate([rope(x[:, :LANE]), rope(x[:, LANE:])], axis=1)

        rq = rope2(p_ref[:, gw:2 * gw])
        rk = rope2(p_ref[:, 2 * gw:3 * gw]) * (HEAD_DIM ** -0.5)
        rkb = rk.astype(BF16)
        rvb = p_ref[:, 3 * gw:4 * gw].astype(BF16)
        yield
        scores = _bdot_nt(head_rows(rq.astype(BF16)), rkb) * dmat_ref[...]
        o_ret = (_bdot(rq * qw_ref[...], s_ret[...])
                 + jnp.dot(side_by_side([scores[h * t:(h + 1) * t] for h in range(nh)]), head_rows(rvb),
                           preferred_element_type=F32))
        kv = _bdot_tn(rk * kw_ref[...], rvb)
        s_ret[...] = s_ret[...] * gc_ref[...] + jnp.where(state_mask, kv, 0.0)
        yield
        o_ret = o_ret * lax.rsqrt(head_sum(o_ret * o_ret) * (1.0 / HEAD_DIM) + NORM_EPS)
        y_ref[:, gw:2 * gw] = (o_ret * _silu(p_ref[:, 4 * gw:5 * gw])).astype(y_ref.dtype)
        yield

        su = _gelu(p_ref[:, 5 * gw:6 * gw])
        sv = _gelu(p_ref[:, 6 * gw:7 * gw])
        mu = jnp.mean(sv, axis=-1, keepdims=True)
        svc = sv - mu
        var = jnp.mean(svc * svc, axis=-1, keepdims=True)
        svn = (svc * lax.rsqrt(var + NORM_EPS) * lng_ref[...] + lnb_ref[...]).astype(BF16)
        s_gate = jnp.dot(wm, head_rows(svn), preferred_element_type=F32) + bs_ref[...]
        y_ref[:, 2 * gw:3 * gw] = (su * s_gate).astype(y_ref.dtype)
        yield

        v_all = jnp.concatenate(v_new, axis=0).astype(BF16)
        o_dn = jnp.concatenate(o_cross, axis=0) + jnp.dot(side_by_side(attn), head_rows(v_all),
                                                          preferred_element_type=F32)
        o_dn = o_dn * lax.rsqrt(head_sum(o_dn * o_dn) * (1.0 / HEAD_DIM) + NORM_EPS) * dng_ref[...]
        y_ref[:, 3 * gw:4 * gw] = (o_dn * _silu(p_ref[:, 10 * gw:11 * gw])).astype(y_ref.dtype)

    for _ in itertools.zip_longest(*[stream(s) for s in range(MIX_STREAMS)], in_proj(x_next, nxt)):
        pass


def _mixers(x3d, g_pre, w_in_all, l, consts, lp):
    batch, seq, d = x3d.shape
    t = MIX_TILE
    ns = MIX_STREAMS
    nt = seq // t
    assert batch % ns == 0 and seq % t == 0, (batch, seq)
    gw = GROUP_WIDTH
    const_inputs = [consts["dmat"], consts["kw"], consts["qw"], consts["gc"], consts["expand"]]
    layer_inputs = [lp["pool_w"], lp["pool_scale"], lp["ln_g"], lp["ln_b"], lp["ws"], lp["bs"],
                    lp["conv_w"], lp["a_log"], lp["dt_bias"], lp["dn_g"]]
    in_specs = ([pl.BlockSpec((ns, t, d), lambda b, i: (b, 0, 0)),
                 pl.BlockSpec((ns, t, d), lambda b, i: (b, jnp.minimum(i + 1, nt - 1), 0)),
                 _resident((1, d)),
                 _resident_layer(w_in_all, l),
                 pl.BlockSpec((t, LANE), lambda b, i: (i, 0)),
                 pl.BlockSpec((t, LANE), lambda b, i: (i, 0))]
                + [_resident(a.shape) for a in const_inputs + layer_inputs])
    y = pl.pallas_call(
        _mix_kernel,
        grid=(batch // ns, nt),
        in_specs=in_specs,
        out_specs=pl.BlockSpec((ns, t, 4 * gw), lambda b, i: (b, i, 0)),
        out_shape=jax.ShapeDtypeStruct((batch, seq, 4 * gw), BF16),
        scratch_shapes=[pltpu.VMEM((2, ns, t, P_PAD), F32),
                        pltpu.VMEM((ns, POOL_HALO + t, gw), F32),
                        pltpu.VMEM((ns, POOL_HALO + t, gw), F32),
                        pltpu.VMEM((ns, POOL_HALO + t, gw), F32),
                        pltpu.VMEM((ns, POOL_HALO + t, LANE), F32),
                        pltpu.VMEM((ns, CONV_HALO + t, 3 * gw), F32),
                        pltpu.VMEM((ns, gw, gw), F32),
                        pltpu.VMEM((ns, gw, gw), F32)],
        compiler_params=pltpu.CompilerParams(dimension_semantics=("arbitrary", "arbitrary"),
                                             vmem_limit_bytes=VMEM_LIMIT),
        name="mixers",
    )(x3d, x3d, g_pre, w_in_all, consts["cos"], consts["sin"], *const_inputs, *layer_inputs)
    return y.reshape(batch * seq, 4 * gw)


def _ffn_kernel(y_ref, x_ref, wout_ref, gmix_ref, gpre_ref, wup_ref, cw_ref, cb_ref, wdn_ref, gpost_ref,
                o_ref, gated_ref, ext_ref, carry_ref, *, tiles_per_seq):
    tm = x_ref.shape[0]
    d_ff = wdn_ref.shape[0]
    tf = FF_TILE
    nf = d_ff // tf

    @pl.when((pl.program_id(0) % tiles_per_seq) == 0)
    def _():
        carry_ref[...] = jnp.zeros_like(carry_ref)

    x = x_ref[...] + _rms_rows(jnp.dot(y_ref[...], wout_ref[...], preferred_element_type=F32)) * gmix_ref[...]
    h = (_rms_rows(x) * gpre_ref[...]).astype(BF16)

    def up(j):
        a = jnp.dot(h, wup_ref[:, j * tf:(j + 1) * tf], preferred_element_type=F32)
        b = jnp.dot(h, wup_ref[:, d_ff + j * tf:d_ff + (j + 1) * tf], preferred_element_type=F32)
        return a, b

    def gate(j, a, b):
        cols = slice(j * tf, (j + 1) * tf)
        slot = j % 2
        ext_ref[slot, 0:CONV_HALO, :] = carry_ref[:, cols]
        ext_ref[slot, CONV_HALO:CONV_HALO + tm, :] = a
        carry_ref[:, cols] = a[tm - CONV_HALO:, :]
        conv = a * cw_ref[FFN_CONV - 1:FFN_CONV, cols] + cb_ref[:, cols]
        for k in range(1, FFN_CONV):
            conv = conv + (ext_ref[slot, pl.ds(CONV_HALO - k, tm), :]
                           * cw_ref[FFN_CONV - 1 - k:FFN_CONV - k, cols])
        return (_gelu(conv) * b).astype(BF16)

    ab = up(0)
    for j in range(nf):
        nxt = up(j + 1) if j + 1 < nf else None
        gated_ref[:, j * tf:(j + 1) * tf] = gate(j, *ab)
        ab = nxt
    z = jnp.dot(gated_ref[...], wdn_ref[...], preferred_element_type=F32)
    o_ref[...] = x + _rms_rows(z) * gpost_ref[...]


def _ffn(y, x2d, seq, l, w_out, g_mix, g_pre, w_up, conv_w, conv_b, w_down, g_post):
    rows, d = x2d.shape
    d_ff = w_down.shape[1]
    tm = min(ROW_TILE, rows)
    return pl.pallas_call(
        functools.partial(_ffn_kernel, tiles_per_seq=seq // tm),
        grid=(rows // tm,),
        in_specs=[pl.BlockSpec((tm, y.shape[1]), lambda i: (i, 0)),
                  pl.BlockSpec((tm, d), lambda i: (i, 0)),
                  _resident_layer(w_out, l),
                  _resident((1, d)),
                  _resident((1, d)),
                  _resident_layer(w_up, l),
                  _resident(conv_w.shape),
                  _resident(conv_b.shape),
                  _resident_layer(w_down, l),
                  _resident((1, d))],
        out_specs=pl.BlockSpec((tm, d), lambda i: (i, 0)),
        out_shape=jax.ShapeDtypeStruct((rows, d), F32),
        scratch_shapes=[pltpu.VMEM((tm, d_ff), BF16),
                        pltpu.VMEM((2, CONV_HALO + tm, FF_TILE), F32),
                        pltpu.VMEM((CONV_HALO, d_ff), F32)],
        compiler_params=pltpu.CompilerParams(dimension_semantics=("arbitrary",),
                                             vmem_limit_bytes=VMEM_LIMIT),
        name="conv_ffn",
    )(y, x2d, w_out, g_mix, g_pre, w_up, conv_w, conv_b, w_down, g_post)


def _shape_constants(seq):
    half = HEAD_DIM // 2
    inv = 1.0 / (ROPE_BASE ** (jnp.arange(0, HEAD_DIM, 2, dtype=F32) / HEAD_DIM))
    ang = jnp.arange(seq, dtype=F32)[:, None] * inv[None, :]
    cos, sin = jnp.cos(ang), jnp.sin(ang)
    cos_l = jnp.tile(cos, (1, LANE // half))
    sin_l = jnp.tile(jnp.concatenate([-sin, sin], axis=1), (1, LANE // HEAD_DIM))
    c = RET_CHUNK
    log_gamma = jnp.log(1.0 - 2.0 ** (-5.0 - jnp.arange(N_HEADS, dtype=F32)))
    pos = jnp.arange(c, dtype=F32)
    diff = pos[:, None] - pos[None, :]
    dmat = jnp.where(diff >= 0, jnp.exp(log_gamma[:, None, None] * jnp.maximum(diff, 0.0)), 0.0)
    k_w = jnp.exp(log_gamma[None, :] * (c - 1.0 - pos)[:, None])
    q_w = jnp.exp(log_gamma[None, :] * (pos + 1.0)[:, None])
    g_chunk = jnp.exp(log_gamma * c)[None, :]
    rep = lambda m: jnp.repeat(m, HEAD_DIM, axis=1)
    src = jnp.arange(LANE)[:, None]
    dst = jnp.arange(2 * GROUP_WIDTH)[None, :]
    expand = (src == dst // HEAD_DIM).astype(BF16)
    return {"cos": cos_l, "sin": sin_l, "dmat": dmat.reshape(N_HEADS * c, c), "kw": rep(k_w),
            "qw": rep(q_w), "gc": rep(g_chunk), "expand": expand}


def _layer_params(l, pool_w, pool_scale, sgu_ln_g, sgu_ln_b, sgu_ws, sgu_bs, dn_conv_w,
                  dn_a_log, dn_dt_bias, dn_norm_g):
    pool_bd = jnp.zeros((GROUP_WIDTH, GROUP_WIDTH), F32)
    for g in range(N_HEADS):
        pool_bd = pool_bd.at[g * HEAD_DIM:(g + 1) * HEAD_DIM, g * HEAD_DIM:(g + 1) * HEAD_DIM].set(pool_w[l, g])
    gate_pad = jnp.zeros((1, LANE), F32)
    return {
        "pool_w": pool_bd.astype(BF16),
        "pool_scale": pool_scale[l][None, :],
        "ln_g": sgu_ln_g[l][None, :],
        "ln_b": sgu_ln_b[l][None, :],
        "ws": sgu_ws[l].transpose(1, 0, 2).reshape(SGU_CHUNK, N_HEADS * SGU_CHUNK),
        "bs": jnp.repeat(sgu_bs[l].T, HEAD_DIM, axis=1),
        "conv_w": dn_conv_w[l],
        "a_log": gate_pad.at[0, N_HEADS:2 * N_HEADS].set(dn_a_log[l]),
        "dt_bias": gate_pad.at[0, N_HEADS:2 * N_HEADS].set(dn_dt_bias[l]),
        "dn_g": jnp.tile(dn_norm_g[l], N_HEADS)[None, :],
    }


def kernel(x, norm_pre_mix, norm_post_mix, norm_pre_ffn, norm_post_ffn, w_in, pool_w, pool_scale, sgu_ln_g, sgu_ln_b, sgu_ws, sgu_bs, dn_conv_w, dn_a_log, dn_dt_bias, dn_norm_g, w_out, ffn_w_up, ffn_conv_w, ffn_conv_b, ffn_w_down):
    batch, seq, d = x.shape
    depth = w_in.shape[0]
    consts = _shape_constants(seq)
    w_in_b = jnp.pad(w_in, ((0, 0), (0, 0), (0, P_PAD - w_in.shape[2]))).astype(BF16)
    w_out_b = w_out.astype(BF16)
    w_up_b = ffn_w_up.astype(BF16)
    w_down_b = ffn_w_down.astype(BF16)
    x2d = x.reshape(batch * seq, d)
    for l in range(depth):
        lp = _layer_params(l, pool_w, pool_scale, sgu_ln_g, sgu_ln_b, sgu_ws, sgu_bs,
                           dn_conv_w, dn_a_log, dn_dt_bias, dn_norm_g)
        y = _mixers(x2d.reshape(batch, seq, d), norm_pre_mix[l][None, :], w_in_b, l, consts, lp)
        x2d = _ffn(y, x2d, seq, l, w_out_b, norm_post_mix[l][None, :], norm_pre_ffn[l][None, :],
                   w_up_b, ffn_conv_w[l], ffn_conv_b[l][None, :], w_down_b, norm_post_ffn[l][None, :])
    return x2d.reshape(batch, seq, d)
```

```python
import functools
import itertools
import math

import jax
import jax.numpy as jnp
from jax import lax
from jax.experimental import pallas as pl
from jax.experimental.pallas import tpu as pltpu

F32 = jnp.float32
BF16 = jnp.bfloat16

NORM_EPS = 1e-6
GELU_C0 = math.sqrt(2.0 / math.pi)
GELU_C1 = 0.044715 * GELU_C0
GROUP_WIDTH = 256
HEAD_DIM = 64
N_HEADS = 4
POOL_WINDOWS = (2, 4, 8, 16)
POOL_HALO = 32
RET_CHUNK = 128
SGU_CHUNK = 128
DN_CHUNK = 64
DN_CONV = 4
NEUMANN_STEPS = 6
CONV_HALO = 8
ROPE_BASE = 10000.0
FFN_CONV = 3
LANE = 128
P_MAIN = 11 * GROUP_WIDTH
P_PAD = P_MAIN + LANE

MIX_TILE = 128
MIX_STREAMS = 4
IN_PROJ_COLS = 256
ROW_TILE = 512
FF_TILE = 256
VMEM_LIMIT = 48 * 1024 * 1024


def _bdot(a, b):
    return jnp.dot(a.astype(BF16), b.astype(BF16), preferred_element_type=F32)


def _bdot_nt(a, b):
    return lax.dot_general(a.astype(BF16), b.astype(BF16), (((1,), (1,)), ((), ())),
                           preferred_element_type=F32)


def _bdot_tn(a, b):
    return lax.dot_general(a.astype(BF16), b.astype(BF16), (((0,), (0,)), ((), ())),
                           preferred_element_type=F32)


def _split(x, terms):
    parts = []
    r = x
    for i in range(terms):
        part = r.astype(BF16)
        parts.append(part)
        if i + 1 < terms:
            r = r - part.astype(F32)
    return parts


def _split_dot(a_bf16, x, terms):
    return sum(jnp.dot(a_bf16, part, preferred_element_type=F32) for part in _split(x, terms))


def _split_dot_rhs(x, b_bf16, terms):
    return sum(jnp.dot(part, b_bf16, preferred_element_type=F32) for part in _split(x, terms))


def _rms_rows(x):
    return x * lax.rsqrt(jnp.mean(x * x, axis=-1, keepdims=True) + NORM_EPS)


def _softplus(x):
    return jnp.maximum(x, 0.0) + jnp.log1p(jnp.exp(-jnp.abs(x)))


def _gelu(x):
    half = 0.5 * x
    inner = x * (GELU_C0 + GELU_C1 * (x * x))
    return half + half * jnp.tanh(inner)


def _silu(x):
    half = 0.5 * x
    return half + half * jnp.tanh(half)


def _resident(shape):
    nd = len(shape)
    return pl.BlockSpec(shape, lambda *_: (0,) * nd, pipeline_mode=pl.Buffered(1))


def _resident_layer(stacked, l):
    nd = stacked.ndim - 1
    return pl.BlockSpec((None,) + stacked.shape[1:], lambda *_: (l,) + (0,) * nd,
                        pipeline_mode=pl.Buffered(1))


def _mix_kernel(x_first, x_next, gpre_ref, win_ref,
                cos_ref, sin_ref, dmat_ref, kw_ref, qw_ref, gc_ref, expand_ref,
                poolw_ref, pools_ref, lng_ref, lnb_ref, ws_ref, bs_ref,
                convw_ref, alog_ref, dtb_ref, dng_ref,
                y_all,
                p_buf, pool_ext_all, pool_s2_all, pool_s4_all, pool_s8_all, conv_ext_all, s_ret_all, s_dn_all):
    t = MIX_TILE
    gw = GROUP_WIDTH
    nh = N_HEADS
    ns = MIX_STREAMS
    cur = pl.program_id(1) % 2
    nxt = 1 - cur

    def in_proj(x_ref, slot):
        x = x_ref[...].reshape(ns * t, x_ref.shape[2])
        h = (_rms_rows(x) * gpre_ref[...]).astype(BF16)
        yield
        for c0 in range(0, P_PAD, IN_PROJ_COLS):
            c1 = min(c0 + IN_PROJ_COLS, P_PAD)
            res = jnp.dot(h, win_ref[:, c0:c1], preferred_element_type=F32)
            for s in range(ns):
                p_buf[slot, s, :, c0:c1] = res[s * t:(s + 1) * t]
            yield

    @pl.when(pl.program_id(1) == 0)
    def _():
        pool_ext_all[:, 0:POOL_HALO, :] = jnp.zeros((MIX_STREAMS, POOL_HALO, gw), F32)
        conv_ext_all[:, 0:CONV_HALO, :] = jnp.zeros((MIX_STREAMS, CONV_HALO, 3 * gw), F32)
        s_ret_all[...] = jnp.zeros_like(s_ret_all)
        s_dn_all[...] = jnp.zeros_like(s_dn_all)
        for _ in in_proj(x_first, 0):
            pass

    lane = lax.broadcasted_iota(jnp.int32, (t, gw), 1)
    head_of_lane = lane >> 6
    mb = [jnp.where(head_of_lane == h, 1.0, 0.0).astype(BF16) for h in range(nh)]
    mb2 = [jnp.concatenate([m, m], axis=1) for m in mb]
    row_t = lax.broadcasted_iota(jnp.int32, (t, gw), 0)
    lane128 = lax.broadcasted_iota(jnp.int32, (t, LANE), 1)
    sr = lax.broadcasted_iota(jnp.int32, (gw, gw), 0) >> 6
    sc = lax.broadcasted_iota(jnp.int32, (gw, gw), 1) >> 6
    state_mask = sr == sc
    head_ones = jnp.where(state_mask, 1.0, 0.0).astype(BF16)
    rr = lax.broadcasted_iota(jnp.int32, (t, t), 0)
    cc = lax.broadcasted_iota(jnp.int32, (t, t), 1)
    same_chunk = (rr >> 6) == (cc >> 6)
    incl = same_chunk & (rr >= cc)
    strict = same_chunk & (rr > cc)
    tril_bd = jnp.where(incl, 1.0, 0.0).astype(BF16)
    eye = jnp.where(rr == cc, 1.0, 0.0)
    zero_tt = jnp.zeros((t, t), BF16)
    r4 = lax.broadcasted_iota(jnp.int32, (t, nh * t), 0)
    c4 = lax.broadcasted_iota(jnp.int32, (t, nh * t), 1) & (t - 1)
    wm = jnp.where(r4 >= c4, ws_ref[...], 0.0).astype(BF16)
    first_half = (lane128 & (HEAD_DIM - 1)) < (HEAD_DIM // 2)
    low = lane128 < HEAD_DIM
    win = jnp.concatenate([jnp.where(low, float(POOL_WINDOWS[0]), float(POOL_WINDOWS[1])),
                           jnp.where(low, float(POOL_WINDOWS[2]), float(POOL_WINDOWS[3]))], axis=1)
    pos = (pl.program_id(1) * t + row_t + 1).astype(F32)
    inv_count = 1.0 / jnp.minimum(pos, win)
    n_chunks = t // DN_CHUNK

    def head_sum(x):
        return jnp.dot(x.astype(BF16), head_ones, preferred_element_type=F32)

    def head_rows(x_bf16):
        masks = mb if x_bf16.shape[1] == gw else mb2
        return jnp.concatenate([x_bf16 * masks[h] for h in range(nh)], axis=0)

    def side_by_side(blocks):
        return jnp.concatenate([b.astype(BF16) for b in blocks], axis=1)

    def stream(s):
        p_ref = p_buf.at[cur, s]
        y_ref = y_all.at[s]
        pool_ext, pool_s2, pool_s4, pool_s8 = (r.at[s] for r in (pool_ext_all, pool_s2_all, pool_s4_all, pool_s8_all))
        conv_ext, s_ret, s_dn = conv_ext_all.at[s], s_ret_all.at[s], s_dn_all.at[s]

        conv_ext[CONV_HALO:CONV_HALO + t, :] = p_ref[:, 7 * gw:10 * gw]
        qkv = None
        for j in range(DN_CONV):
            term = conv_ext[pl.ds(CONV_HALO - (DN_CONV - 1) + j, t), :] * convw_ref[j:j + 1, :]
            qkv = term if qkv is None else qkv + term
        conv_ext[0:CONV_HALO, :] = conv_ext[t:t + CONV_HALO, :]
        qkv = _silu(qkv)
        dq = qkv[:, 0:gw]
        dk = qkv[:, gw:2 * gw]
        dv = qkv[:, 2 * gw:3 * gw]
        yield
        dq = dq * lax.rsqrt(head_sum(dq * dq) + NORM_EPS) * (HEAD_DIM ** -0.5)
        dk = dk * lax.rsqrt(head_sum(dk * dk) + NORM_EPS)
        yield

        ba = p_ref[:, P_MAIN:P_PAD]
        g_c = -jnp.exp(alog_ref[...]) * _softplus(ba + dtb_ref[...])
        decay_c = _split_dot(tril_bd, g_c, 3)
        gates = jnp.where(lane128 < nh, jax.nn.sigmoid(ba), decay_c)
        gates = _split_dot_rhs(gates, expand_ref[...], 3)
        beta = gates[:, 0:gw]
        decay = gates[:, gw:2 * gw]
        decay_rows = decay_c.T
        yield

        exp_decay = jnp.exp(decay)
        k_beta = dk * beta
        rhs = jnp.concatenate([dv * beta, k_beta * exp_decay], axis=1).astype(BF16)
        q_dec = dq * exp_decay
        d_last = [decay[(c + 1) * DN_CHUNK - 1:(c + 1) * DN_CHUNK, :] for c in range(n_chunks)]
        d_last_rows = jnp.concatenate([jnp.broadcast_to(dl, (DN_CHUNK, gw)) for dl in d_last], axis=0)
        k_tail = dk * jnp.exp(d_last_rows - decay)
        dkb = dk.astype(BF16)
        kq = _bdot_nt(jnp.concatenate([head_rows(k_beta.astype(BF16)), head_rows(dq.astype(BF16))], axis=0),
                      dkb)
        yield
        series = []
        attn = []
        for h in range(nh):
            col = jnp.broadcast_to(decay_c[:, nh + h:nh + h + 1], (t, t))
            row = jnp.broadcast_to(decay_rows[nh + h:nh + h + 1, :], (t, t))
            dmask = jnp.where(incl, jnp.exp(col - row), 0.0)
            series.append(jnp.where(strict, kq[h * t:(h + 1) * t] * (-dmask), 0.0))
            attn.append(kq[(nh + h) * t:(nh + h + 1) * t] * dmask)
        yield
        inv = [eye + sk for sk in series]
        series = [sk.astype(BF16) for sk in series]
        for k in range(NEUMANN_STEPS):
            first, last = k == 0, k == NEUMANN_STEPS - 1
            for h in range(0, nh, 2):
                s_pair = jnp.concatenate(series[h:h + 2], axis=1)
                block_diag = jnp.concatenate(
                    [jnp.concatenate([series[h], zero_tt], axis=1),
                     jnp.concatenate([zero_tt, series[h + 1]], axis=1)], axis=0)
                a_pair = None if first else jnp.concatenate([inv[h].astype(BF16), inv[h + 1].astype(BF16)], axis=1)
                lhs = [x for x in (a_pair, None if last else s_pair) if x is not None]
                out = jnp.dot(jnp.concatenate(lhs, axis=0), block_diag, preferred_element_type=F32)
                if not first:
                    inv[h] = inv[h] + out[0:t, 0:t]
                    inv[h + 1] = inv[h + 1] + out[0:t, t:2 * t]
                if not last:
                    r0 = 0 if first else t
                    series[h] = out[r0:r0 + t, 0:t].astype(BF16)
                    series[h + 1] = out[r0:r0 + t, t:2 * t].astype(BF16)
            yield
        sol = jnp.dot(side_by_side(inv), head_rows(rhs), preferred_element_type=F32)
        u = sol[:, 0:gw]
        w = sol[:, gw:2 * gw]
        yield

        a = p_ref[:, 0:gw]
        n = POOL_HALO + t
        pool_ext[POOL_HALO:n, :] = a
        s2 = pool_ext[8:n, :] + pool_ext[pl.ds(7, n - 8), :]
        pool_s2[8:n, :] = s2
        s4 = pool_s2[16:n, :] + pool_s2[pl.ds(14, n - 16), :]
        pool_s4[16:n, :] = s4
        s8 = pool_s4[24:n, LANE:] + pool_s4[pl.ds(20, n - 24), LANE:]
        pool_s8[24:n, :] = s8
        s16 = pool_s8[32:n, :] + pool_s8[24:n - 8, :]
        pool_ext[0:POOL_HALO, :] = pool_ext[t:n, :]
        win_sum = jnp.concatenate([jnp.where(low, s2[24:, 0:LANE], s4[16:, 0:LANE]),
                                   jnp.where(low, s8[8:, :], s16)], axis=1)
        d_pool = win_sum * inv_count - a
        y_a = _bdot(d_pool, poolw_ref[...]) * pools_ref[...]
        y_ref[:, 0:gw] = y_a.astype(y_ref.dtype)
        yield

        state = s_dn[...]
        v_new = []
        o_cross = []
        for c in range(n_chunks):
            r0, r1 = c * DN_CHUNK, (c + 1) * DN_CHUNK
            sb = state.astype(BF16)
            vn = u[r0:r1] - _bdot(w[r0:r1], sb)
            o_cross.append(_bdot(q_dec[r0:r1], sb))
            state = state * jnp.exp(d_last[c]) + jnp.where(state_mask, _bdot_tn(k_tail[r0:r1], vn), 0.0)
            v_new.append(vn)
            yield
        s_dn[...] = state

        cos = cos_ref[...]
        sin = sin_ref[...]

        def rope(xh):
            partner = jnp.where(first_half, pltpu.roll(xh, LANE - HEAD_DIM // 2, 1),
                                pltpu.roll(xh, HEAD_DIM // 2, 1))
            return xh * cos + partner * sin

        def rope2(x):
            return jnp.concatenate([rope(x[:, :LANE]), rope(x[:, LANE:])], axis=1)

        rq = rope2(p_ref[:, gw:2 * gw])
        rk = rope2(p_ref[:, 2 * gw:3 * gw]) * (HEAD_DIM ** -0.5)
        rkb = rk.astype(BF16)
        rvb = p_ref[:, 3 * gw:4 * gw].astype(BF16)
        yield
        scores = _bdot_nt(head_rows(rq.astype(BF16)), rkb) * dmat_ref[...]
        o_ret = (_bdot(rq * qw_ref[...], s_ret[...])
                 + jnp.dot(side_by_side([scores[h * t:(h + 1) * t] for h in range(nh)]), head_rows(rvb),
                           preferred_element_type=F32))
        kv = _bdot_tn(rk * kw_ref[...], rvb)
        s_ret[...] = s_ret[...] * gc_ref[...] + jnp.where(state_mask, kv, 0.0)
        yield
        o_ret = o_ret * lax.rsqrt(head_sum(o_ret * o_ret) * (1.0 / HEAD_DIM) + NORM_EPS)
        y_ref[:, gw:2 * gw] = (o_ret * _silu(p_ref[:, 4 * gw:5 * gw])).astype(y_ref.dtype)
        yield

        su = _gelu(p_ref[:, 5 * gw:6 * gw])
        sv = _gelu(p_ref[:, 6 * gw:7 * gw])
        mu = jnp.mean(sv, axis=-1, keepdims=True)
        svc = sv - mu
        var = jnp.mean(svc * svc, axis=-1, keepdims=True)
        svn = (svc * lax.rsqrt(var + NORM_EPS) * lng_ref[...] + lnb_ref[...]).astype(BF16)
        s_gate = jnp.dot(wm, head_rows(svn), preferred_element_type=F32) + bs_ref[...]
        y_ref[:, 2 * gw:3 * gw] = (su * s_gate).astype(y_ref.dtype)
        yield

        v_all = jnp.concatenate(v_new, axis=0).astype(BF16)
        o_dn = jnp.concatenate(o_cross, axis=0) + jnp.dot(side_by_side(attn), head_rows(v_all),
                                                          preferred_element_type=F32)
        o_dn = o_dn * lax.rsqrt(head_sum(o_dn * o_dn) * (1.0 / HEAD_DIM) + NORM_EPS) * dng_ref[...]
        y_ref[:, 3 * gw:4 * gw] = (o_dn * _silu(p_ref[:, 10 * gw:11 * gw])).astype(y_ref.dtype)

    for _ in itertools.zip_longest(*[stream(s) for s in range(MIX_STREAMS)], in_proj(x_next, nxt)):
        pass


def _mixers(x3d, g_pre, w_in_all, l, consts, lp):
    batch, seq, d = x3d.shape
    t = MIX_TILE
    ns = MIX_STREAMS
    nt = seq // t
    assert batch % ns == 0 and seq % t == 0, (batch, seq)
    gw = GROUP_WIDTH
    const_inputs = [consts["dmat"], consts["kw"], consts["qw"], consts["gc"], consts["expand"]]
    layer_inputs = [lp["pool_w"], lp["pool_scale"], lp["ln_g"], lp["ln_b"], lp["ws"], lp["bs"],
                    lp["conv_w"], lp["a_log"], lp["dt_bias"], lp["dn_g"]]
    in_specs = ([pl.BlockSpec((ns, t, d), lambda b, i: (b, 0, 0)),
                 pl.BlockSpec((ns, t, d), lambda b, i: (b, jnp.minimum(i + 1, nt - 1), 0)),
                 _resident((1, d)),
                 _resident_layer(w_in_all, l),
                 pl.BlockSpec((t, LANE), lambda b, i: (i, 0)),
                 pl.BlockSpec((t, LANE), lambda b, i: (i, 0))]
                + [_resident(a.shape) for a in const_inputs + layer_inputs])
    y = pl.pallas_call(
        _mix_kernel,
        grid=(batch // ns, nt),
        in_specs=in_specs,
        out_specs=pl.BlockSpec((ns, t, 4 * gw), lambda b, i: (b, i, 0)),
        out_shape=jax.ShapeDtypeStruct((batch, seq, 4 * gw), BF16),
        scratch_shapes=[pltpu.VMEM((2, ns, t, P_PAD), F32),
                        pltpu.VMEM((ns, POOL_HALO + t, gw), F32),
                        pltpu.VMEM((ns, POOL_HALO + t, gw), F32),
                        pltpu.VMEM((ns, POOL_HALO + t, gw), F32),
                        pltpu.VMEM((ns, POOL_HALO + t, LANE), F32),
                        pltpu.VMEM((ns, CONV_HALO + t, 3 * gw), F32),
                        pltpu.VMEM((ns, gw, gw), F32),
                        pltpu.VMEM((ns, gw, gw), F32)],
        compiler_params=pltpu.CompilerParams(dimension_semantics=("arbitrary", "arbitrary"),
                                             vmem_limit_bytes=VMEM_LIMIT),
        name="mixers",
    )(x3d, x3d, g_pre, w_in_all, consts["cos"], consts["sin"], *const_inputs, *layer_inputs)
    return y.reshape(batch * seq, 4 * gw)


def _ffn_kernel(y_ref, x_ref, wout_ref, gmix_ref, gpre_ref, wup_ref, cw_ref, cb_ref, wdn_ref, gpost_ref,
                o_ref, gated_ref, ext_ref, carry_ref, *, tiles_per_seq):
    tm = x_ref.shape[0]
    d_ff = wdn_ref.shape[0]
    tf = FF_TILE
    nf = d_ff // tf

    @pl.when((pl.program_id(0) % tiles_per_seq) == 0)
    def _():
        carry_ref[...] = jnp.zeros_like(carry_ref)

    x = x_ref[...] + _rms_rows(jnp.dot(y_ref[...], wout_ref[...], preferred_element_type=F32)) * gmix_ref[...]
    h = (_rms_rows(x) * gpre_ref[...]).astype(BF16)

    def up(j):
        a = jnp.dot(h, wup_ref[:, j * tf:(j + 1) * tf], preferred_element_type=F32)
        b = jnp.dot(h, wup_ref[:, d_ff + j * tf:d_ff + (j + 1) * tf], preferred_element_type=F32)
        return a, b

    def gate(j, a, b):
        cols = slice(j * tf, (j + 1) * tf)
        slot = j % 2
        ext_ref[slot, 0:CONV_HALO, :] = carry_ref[:, cols]
        ext_ref[slot, CONV_HALO:CONV_HALO + tm, :] = a
        carry_ref[:, cols] = a[tm - CONV_HALO:, :]
        conv = a * cw_ref[FFN_CONV - 1:FFN_CONV, cols] + cb_ref[:, cols]
        for k in range(1, FFN_CONV):
            conv = conv + (ext_ref[slot, pl.ds(CONV_HALO - k, tm), :]
                           * cw_ref[FFN_CONV - 1 - k:FFN_CONV - k, cols])
        return (_gelu(conv) * b).astype(BF16)

    ab = up(0)
    for j in range(nf):
        nxt = up(j + 1) if j + 1 < nf else None
        gated_ref[:, j * tf:(j + 1) * tf] = gate(j, *ab)
        ab = nxt
    z = jnp.dot(gated_ref[...], wdn_ref[...], preferred_element_type=F32)
    o_ref[...] = x + _rms_rows(z) * gpost_ref[...]


def _ffn(y, x2d, seq, l, w_out, g_mix, g_pre, w_up, conv_w, conv_b, w_down, g_post):
    rows, d = x2d.shape
    d_ff = w_down.shape[1]
    tm = min(ROW_TILE, rows)
    return pl.pallas_call(
        functools.partial(_ffn_kernel, tiles_per_seq=seq // tm),
        grid=(rows // tm,),
        in_specs=[pl.BlockSpec((tm, y.shape[1]), lambda i: (i, 0)),
                  pl.BlockSpec((tm, d), lambda i: (i, 0)),
                  _resident_layer(w_out, l),
                  _resident((1, d)),
                  _resident((1, d)),
                  _resident_layer(w_up, l),
                  _resident(conv_w.shape),
                  _resident(conv_b.shape),
                  _resident_layer(w_down, l),
                  _resident((1, d))],
        out_specs=pl.BlockSpec((tm, d), lambda i: (i, 0)),
        out_shape=jax.ShapeDtypeStruct((rows, d), F32),
        scratch_shapes=[pltpu.VMEM((tm, d_ff), BF16),
                        pltpu.VMEM((2, CONV_HALO + tm, FF_TILE), F32),
                        pltpu.VMEM((CONV_HALO, d_ff), F32)],
        compiler_params=pltpu.CompilerParams(dimension_semantics=("arbitrary",),
                                             vmem_limit_bytes=VMEM_LIMIT),
        name="conv_ffn",
    )(y, x2d, w_out, g_mix, g_pre, w_up, conv_w, conv_b, w_down, g_post)


def _shape_constants(seq):
    half = HEAD_DIM // 2
    inv = 1.0 / (ROPE_BASE ** (jnp.arange(0, HEAD_DIM, 2, dtype=F32) / HEAD_DIM))
    ang = jnp.arange(seq, dtype=F32)[:, None] * inv[None, :]
    cos, sin = jnp.cos(ang), jnp.sin(ang)
    cos_l = jnp.tile(cos, (1, LANE // half))
    sin_l = jnp.tile(jnp.concatenate([-sin, sin], axis=1), (1, LANE // HEAD_DIM))
    c = RET_CHUNK
    log_gamma = jnp.log(1.0 - 2.0 ** (-5.0 - jnp.arange(N_HEADS, dtype=F32)))
    pos = jnp.arange(c, dtype=F32)
    diff = pos[:, None] - pos[None, :]
    dmat = jnp.where(diff >= 0, jnp.exp(log_gamma[:, None, None] * jnp.maximum(diff, 0.0)), 0.0)
    k_w = jnp.exp(log_gamma[None, :] * (c - 1.0 - pos)[:, None])
    q_w = jnp.exp(log_gamma[None, :] * (pos + 1.0)[:, None])
    g_chunk = jnp.exp(log_gamma * c)[None, :]
    rep = lambda m: jnp.repeat(m, HEAD_DIM, axis=1)
    src = jnp.arange(LANE)[:, None]
    dst = jnp.arange(2 * GROUP_WIDTH)[None, :]
    expand = (src == dst // HEAD_DIM).astype(BF16)
    return {"cos": cos_l, "sin": sin_l, "dmat": dmat.reshape(N_HEADS * c, c), "kw": rep(k_w),
            "qw": rep(q_w), "gc": rep(g_chunk), "expand": expand}


def _layer_params(l, pool_w, pool_scale, sgu_ln_g, sgu_ln_b, sgu_ws, sgu_bs, dn_conv_w,
                  dn_a_log, dn_dt_bias, dn_norm_g):
    pool_bd = jnp.zeros((GROUP_WIDTH, GROUP_WIDTH), F32)
    for g in range(N_HEADS):
        pool_bd = pool_bd.at[g * HEAD_DIM:(g + 1) * HEAD_DIM, g * HEAD_DIM:(g + 1) * HEAD_DIM].set(pool_w[l, g])
    gate_pad = jnp.zeros((1, LANE), F32)
    return {
        "pool_w": pool_bd.astype(BF16),
        "pool_scale": pool_scale[l][None, :],
        "ln_g": sgu_ln_g[l][None, :],
        "ln_b": sgu_ln_b[l][None, :],
        "ws": sgu_ws[l].transpose(1, 0, 2).reshape(SGU_CHUNK, N_HEADS * SGU_CHUNK),
        "bs": jnp.repeat(sgu_bs[l].T, HEAD_DIM, axis=1),
        "conv_w": dn_conv_w[l],
        "a_log": gate_pad.at[0, N_HEADS:2 * N_HEADS].set(dn_a_log[l]),
        "dt_bias": gate_pad.at[0, N_HEADS:2 * N_HEADS].set(dn_dt_bias[l]),
        "dn_g": jnp.tile(dn_norm_g[l], N_HEADS)[None, :],
    }


def kernel(x, norm_pre_mix, norm_post_mix, norm_pre_ffn, norm_post_ffn, w_in, pool_w, pool_scale, sgu_ln_g, sgu_ln_b, sgu_ws, sgu_bs, dn_conv_w, dn_a_log, dn_dt_bias, dn_norm_g, w_out, ffn_w_up, ffn_conv_w, ffn_conv_b, ffn_w_down):
    batch, seq, d = x.shape
    depth = w_in.shape[0]
    consts = _shape_constants(seq)
    w_in_b = jnp.pad(w_in, ((0, 0), (0, 0), (0, P_PAD - w_in.shape[2]))).astype(BF16)
    w_out_b = w_out.astype(BF16)
    w_up_b = ffn_w_up.astype(BF16)
    w_down_b = ffn_w_down.astype(BF16)
    x2d = x.reshape(batch * seq, d)
    for l in range(depth):
        lp = _layer_params(l, pool_w, pool_scale, sgu_ln_g, sgu_ln_b, sgu_ws, sgu_bs,
                           dn_conv_w, dn_a_log, dn_dt_bias, dn_norm_g)
        y = _mixers(x2d.reshape(batch, seq, d), norm_pre_mix[l][None, :], w_in_b, l, consts, lp)
        x2d = _ffn(y, x2d, seq, l, w_out_b, norm_post_mix[l][None, :], norm_pre_ffn[l][None, :],
                   w_up_b, ffn_conv_w[l], ffn_conv_b[l][None, :], w_down_b, norm_post_ffn[l][None, :])
    return x2d.reshape(batch, seq, d)
```

```python
import functools
import itertools
import math

import jax
import jax.numpy as jnp
from jax import lax
from jax.experimental import pallas as pl
from jax.experimental.pallas import tpu as pltpu

F32 = jnp.float32
BF16 = jnp.bfloat16

NORM_EPS = 1e-6
GELU_C0 = math.sqrt(2.0 / math.pi)
GELU_C1 = 0.044715 * GELU_C0
GROUP_WIDTH = 256
HEAD_DIM = 64
N_HEADS = 4
POOL_WINDOWS = (2, 4, 8, 16)
POOL_HALO = 32
RET_CHUNK = 128
SGU_CHUNK = 128
DN_CHUNK = 64
DN_CONV = 4
NEUMANN_STEPS = 6
CONV_HALO = 8
ROPE_BASE = 10000.0
FFN_CONV = 3
LANE = 128
P_MAIN = 11 * GROUP_WIDTH
P_PAD = P_MAIN + LANE

MIX_TILE = 128
MIX_STREAMS = 4
IN_PROJ_COLS = 256
ROW_TILE = 512
FF_TILE = 256
FFN_TILES = 2
FFN_SKEW = 4
VMEM_LIMIT = 48 * 1024 * 1024
FFN_VMEM_LIMIT = 58 * 1024 * 1024


def _bdot(a, b):
    return jnp.dot(a.astype(BF16), b.astype(BF16), preferred_element_type=F32)


def _bdot_nt(a, b):
    return lax.dot_general(a.astype(BF16), b.astype(BF16), (((1,), (1,)), ((), ())),
                           preferred_element_type=F32)


def _bdot_tn(a, b):
    return lax.dot_general(a.astype(BF16), b.astype(BF16), (((0,), (0,)), ((), ())),
                           preferred_element_type=F32)


def _split(x, terms):
    parts = []
    r = x
    for i in range(terms):
        part = r.astype(BF16)
        parts.append(part)
        if i + 1 < terms:
            r = r - part.astype(F32)
    return parts


def _split_dot(a_bf16, x, terms):
    return sum(jnp.dot(a_bf16, part, preferred_element_type=F32) for part in _split(x, terms))


def _split_dot_rhs(x, b_bf16, terms):
    return sum(jnp.dot(part, b_bf16, preferred_element_type=F32) for part in _split(x, terms))


def _rms_rows(x):
    return x * lax.rsqrt(jnp.mean(x * x, axis=-1, keepdims=True) + NORM_EPS)


def _softplus(x):
    return jnp.maximum(x, 0.0) + jnp.log1p(jnp.exp(-jnp.abs(x)))


def _gelu(x):
    half = 0.5 * x
    inner = x * (GELU_C0 + GELU_C1 * (x * x))
    return half + half * jnp.tanh(inner)


def _silu(x):
    half = 0.5 * x
    return half + half * jnp.tanh(half)


def _resident(shape):
    nd = len(shape)
    return pl.BlockSpec(shape, lambda *_: (0,) * nd, pipeline_mode=pl.Buffered(1))


def _resident_layer(stacked, l):
    nd = stacked.ndim - 1
    return pl.BlockSpec((None,) + stacked.shape[1:], lambda *_: (l,) + (0,) * nd,
                        pipeline_mode=pl.Buffered(1))


def _mix_kernel(x_first, x_next, gpre_ref, win_ref,
                cos_ref, sin_ref, dmat_ref, kw_ref, qw_ref, gc_ref, expand_ref,
                poolw_ref, pools_ref, lng_ref, lnb_ref, ws_ref, bs_ref,
                convw_ref, alog_ref, dtb_ref, dng_ref,
                y_all,
                p_buf, pool_ext_all, pool_s2_all, pool_s4_all, pool_s8_all, conv_ext_all, s_ret_all, s_dn_all):
    t = MIX_TILE
    gw = GROUP_WIDTH
    nh = N_HEADS
    ns = MIX_STREAMS
    cur = pl.program_id(1) % 2
    nxt = 1 - cur

    def in_proj(x_ref, slot):
        x = x_ref[...].reshape(ns * t, x_ref.shape[2])
        h = (_rms_rows(x) * gpre_ref[...]).astype(BF16)
        yield
        for c0 in range(0, P_PAD, IN_PROJ_COLS):
            c1 = min(c0 + IN_PROJ_COLS, P_PAD)
            res = jnp.dot(h, win_ref[:, c0:c1], preferred_element_type=F32)
            for s in range(ns):
                p_buf[slot, s, :, c0:c1] = res[s * t:(s + 1) * t]
            yield

    @pl.when(pl.program_id(1) == 0)
    def _():
        pool_ext_all[:, 0:POOL_HALO, :] = jnp.zeros((MIX_STREAMS, POOL_HALO, gw), F32)
        conv_ext_all[:, 0:CONV_HALO, :] = jnp.zeros((MIX_STREAMS, CONV_HALO, 3 * gw), F32)
        s_ret_all[...] = jnp.zeros_like(s_ret_all)
        s_dn_all[...] = jnp.zeros_like(s_dn_all)
        for _ in in_proj(x_first, 0):
            pass

    lane = lax.broadcasted_iota(jnp.int32, (t, gw), 1)
    head_of_lane = lane >> 6
    mb = [jnp.where(head_of_lane == h, 1.0, 0.0).astype(BF16) for h in range(nh)]
    mb2 = [jnp.concatenate([m, m], axis=1) for m in mb]
    row_t = lax.broadcasted_iota(jnp.int32, (t, gw), 0)
    lane128 = lax.broadcasted_iota(jnp.int32, (t, LANE), 1)
    sr = lax.broadcasted_iota(jnp.int32, (gw, gw), 0) >> 6
    sc = lax.broadcasted_iota(jnp.int32, (gw, gw), 1) >> 6
    state_mask = sr == sc
    head_ones = jnp.where(state_mask, 1.0, 0.0).astype(BF16)
    rr = lax.broadcasted_iota(jnp.int32, (t, t), 0)
    cc = lax.broadcasted_iota(jnp.int32, (t, t), 1)
    same_chunk = (rr >> 6) == (cc >> 6)
    incl = same_chunk & (rr >= cc)
    strict = same_chunk & (rr > cc)
    tril_bd = jnp.where(incl, 1.0, 0.0).astype(BF16)
    eye = jnp.where(rr == cc, 1.0, 0.0)
    zero_tt = jnp.zeros((t, t), BF16)
    r4 = lax.broadcasted_iota(jnp.int32, (t, nh * t), 0)
    c4 = lax.broadcasted_iota(jnp.int32, (t, nh * t), 1) & (t - 1)
    wm = jnp.where(r4 >= c4, ws_ref[...], 0.0).astype(BF16)
    first_half = (lane128 & (HEAD_DIM - 1)) < (HEAD_DIM // 2)
    low = lane128 < HEAD_DIM
    win = jnp.concatenate([jnp.where(low, float(POOL_WINDOWS[0]), float(POOL_WINDOWS[1])),
                           jnp.where(low, float(POOL_WINDOWS[2]), float(POOL_WINDOWS[3]))], axis=1)
    pos = (pl.program_id(1) * t + row_t + 1).astype(F32)
    inv_count = 1.0 / jnp.minimum(pos, win)
    n_chunks = t // DN_CHUNK

    def head_sum(x):
        return jnp.dot(x.astype(BF16), head_ones, preferred_element_type=F32)

    def head_rows(x_bf16):
        masks = mb if x_bf16.shape[1] == gw else mb2
        return jnp.concatenate([x_bf16 * masks[h] for h in range(nh)], axis=0)

    def side_by_side(blocks):
        return jnp.concatenate([b.astype(BF16) for b in blocks], axis=1)

    def stream(s):
        p_ref = p_buf.at[cur, s]
        y_ref = y_all.at[s]
        pool_ext, pool_s2, pool_s4, pool_s8 = (r.at[s] for r in (pool_ext_all, pool_s2_all, pool_s4_all, pool_s8_all))
        conv_ext, s_ret, s_dn = conv_ext_all.at[s], s_ret_all.at[s], s_dn_all.at[s]

        conv_ext[CONV_HALO:CONV_HALO + t, :] = p_ref[:, 7 * gw:10 * gw]
        qkv = None
        for j in range(DN_CONV):
            term = conv_ext[pl.ds(CONV_HALO - (DN_CONV - 1) + j, t), :] * convw_ref[j:j + 1, :]
            qkv = term if qkv is None else qkv + term
        conv_ext[0:CONV_HALO, :] = conv_ext[t:t + CONV_HALO, :]
        qkv = _silu(qkv)
        dq = qkv[:, 0:gw]
        dk = qkv[:, gw:2 * gw]
        dv = qkv[:, 2 * gw:3 * gw]
        yield
        dq = dq * lax.rsqrt(head_sum(dq * dq) + NORM_EPS) * (HEAD_DIM ** -0.5)
        dk = dk * lax.rsqrt(head_sum(dk * dk) + NORM_EPS)
        yield

        ba = p_ref[:, P_MAIN:P_PAD]
        g_c = -jnp.exp(alog_ref[...]) * _softplus(ba + dtb_ref[...])
        decay_c = _split_dot(tril_bd, g_c, 3)
        gates = jnp.where(lane128 < nh, jax.nn.sigmoid(ba), decay_c)
        gates = _split_dot_rhs(gates, expand_ref[...], 3)
        beta = gates[:, 0:gw]
        decay = gates[:, gw:2 * gw]
        decay_rows = decay_c.T
        yield

        exp_decay = jnp.exp(decay)
        k_beta = dk * beta
        rhs = jnp.concatenate([dv * beta, k_beta * exp_decay], axis=1).astype(BF16)
        q_dec = dq * exp_decay
        d_last = [decay[(c + 1) * DN_CHUNK - 1:(c + 1) * DN_CHUNK, :] for c in range(n_chunks)]
        d_last_rows = jnp.concatenate([jnp.broadcast_to(dl, (DN_CHUNK, gw)) for dl in d_last], axis=0)
        k_tail = dk * jnp.exp(d_last_rows - decay)
        dkb = dk.astype(BF16)
        kq = _bdot_nt(jnp.concatenate([head_rows(k_beta.astype(BF16)), head_rows(dq.astype(BF16))], axis=0),
                      dkb)
        yield
        series = []
        attn = []
        for h in range(nh):
            col = jnp.broadcast_to(decay_c[:, nh + h:nh + h + 1], (t, t))
            row = jnp.broadcast_to(decay_rows[nh + h:nh + h + 1, :], (t, t))
            dmask = jnp.where(incl, jnp.exp(col - row), 0.0)
            series.append(jnp.where(strict, kq[h * t:(h + 1) * t] * (-dmask), 0.0))
            attn.append(kq[(nh + h) * t:(nh + h + 1) * t] * dmask)
        yield
        inv = [eye + sk for sk in series]
        series = [sk.astype(BF16) for sk in series]
        for k in range(NEUMANN_STEPS):
            first, last = k == 0, k == NEUMANN_STEPS - 1
            for h in range(0, nh, 2):
                s_pair = jnp.concatenate(series[h:h + 2], axis=1)
                block_diag = jnp.concatenate(
                    [jnp.concatenate([series[h], zero_tt], axis=1),
                     jnp.concatenate([zero_tt, series[h + 1]], axis=1)], axis=0)
                a_pair = None if first else jnp.concatenate([inv[h].astype(BF16), inv[h + 1].astype(BF16)], axis=1)
                lhs = [x for x in (a_pair, None if last else s_pair) if x is not None]
                out = jnp.dot(jnp.concatenate(lhs, axis=0), block_diag, preferred_element_type=F32)
                if not first:
                    inv[h] = inv[h] + out[0:t, 0:t]
                    inv[h + 1] = inv[h + 1] + out[0:t, t:2 * t]
                if not last:
                    r0 = 0 if first else t
                    series[h] = out[r0:r0 + t, 0:t].astype(BF16)
                    series[h + 1] = out[r0:r0 + t, t:2 * t].astype(BF16)
            yield
        sol = jnp.dot(side_by_side(inv), head_rows(rhs), preferred_element_type=F32)
        u = sol[:, 0:gw]
        w = sol[:, gw:2 * gw]
        yield

        a = p_ref[:, 0:gw]
        n = POOL_HALO + t
        pool_ext[POOL_HALO:n, :] = a
        s2 = pool_ext[8:n, :] + pool_ext[pl.ds(7, n - 8), :]
        pool_s2[8:n, :] = s2
        s4 = pool_s2[16:n, :] + pool_s2[pl.ds(14, n - 16), :]
        pool_s4[16:n, :] = s4
        s8 = pool_s4[24:n, LANE:] + pool_s4[pl.ds(20, n - 24), LANE:]
        pool_s8[24:n, :] = s8
        s16 = pool_s8[32:n, :] + pool_s8[24:n - 8, :]
        pool_ext[0:POOL_HALO, :] = pool_ext[t:n, :]
        win_sum = jnp.concatenate([jnp.where(low, s2[24:, 0:LANE], s4[16:, 0:LANE]),
                                   jnp.where(low, s8[8:, :], s16)], axis=1)
        d_pool = win_sum * inv_count - a
        y_a = _bdot(d_pool, poolw_ref[...]) * pools_ref[...]
        y_ref[:, 0:gw] = y_a.astype(y_ref.dtype)
        yield

        state = s_dn[...]
        v_new = []
        o_cross = []
        for c in range(n_chunks):
            r0, r1 = c * DN_CHUNK, (c + 1) * DN_CHUNK
            sb = state.astype(BF16)
            vn = u[r0:r1] - _bdot(w[r0:r1], sb)
            o_cross.append(_bdot(q_dec[r0:r1], sb))
            state = state * jnp.exp(d_last[c]) + jnp.where(state_mask, _bdot_tn(k_tail[r0:r1], vn), 0.0)
            v_new.append(vn)
            yield
        s_dn[...] = state

        cos = cos_ref[...]
        sin = sin_ref[...]

        def rope(xh):
            partner = jnp.where(first_half, pltpu.roll(xh, LANE - HEAD_DIM // 2, 1),
                                pltpu.roll(xh, HEAD_DIM // 2, 1))
            return xh * cos + partner * sin

        def rope2(x):
            return jnp.concatenate([rope(x[:, :LANE]), rope(x[:, LANE:])], axis=1)

        rq = rope2(p_ref[:, gw:2 * gw])
        rk = rope2(p_ref[:, 2 * gw:3 * gw]) * (HEAD_DIM ** -0.5)
        rkb = rk.astype(BF16)
        rvb = p_ref[:, 3 * gw:4 * gw].astype(BF16)
        yield
        scores = _bdot_nt(head_rows(rq.astype(BF16)), rkb) * dmat_ref[...]
        o_ret = (_bdot(rq * qw_ref[...], s_ret[...])
                 + jnp.dot(side_by_side([scores[h * t:(h + 1) * t] for h in range(nh)]), head_rows(rvb),
                           preferred_element_type=F32))
        kv = _bdot_tn(rk * kw_ref[...], rvb)
        s_ret[...] = s_ret[...] * gc_ref[...] + jnp.where(state_mask, kv, 0.0)
        yield
        o_ret = o_ret * lax.rsqrt(head_sum(o_ret * o_ret) * (1.0 / HEAD_DIM) + NORM_EPS)
        y_ref[:, gw:2 * gw] = (o_ret * _silu(p_ref[:, 4 * gw:5 * gw])).astype(y_ref.dtype)
        yield

        su = _gelu(p_ref[:, 5 * gw:6 * gw])
        sv = _gelu(p_ref[:, 6 * gw:7 * gw])
        mu = jnp.mean(sv, axis=-1, keepdims=True)
        svc = sv - mu
        var = jnp.mean(svc * svc, axis=-1, keepdims=True)
        svn = (svc * lax.rsqrt(var + NORM_EPS) * lng_ref[...] + lnb_ref[...]).astype(BF16)
        s_gate = jnp.dot(wm, head_rows(svn), preferred_element_type=F32) + bs_ref[...]
        y_ref[:, 2 * gw:3 * gw] = (su * s_gate).astype(y_ref.dtype)
        yield

        v_all = jnp.concatenate(v_new, axis=0).astype(BF16)
        o_dn = jnp.concatenate(o_cross, axis=0) + jnp.dot(side_by_side(attn), head_rows(v_all),
                                                          preferred_element_type=F32)
        o_dn = o_dn * lax.rsqrt(head_sum(o_dn * o_dn) * (1.0 / HEAD_DIM) + NORM_EPS) * dng_ref[...]
        y_ref[:, 3 * gw:4 * gw] = (o_dn * _silu(p_ref[:, 10 * gw:11 * gw])).astype(y_ref.dtype)

    for _ in itertools.zip_longest(*[stream(s) for s in range(MIX_STREAMS)], in_proj(x_next, nxt)):
        pass


def _mixers(x3d, l, g_pre_all, w_in_all, consts, layer_inputs):
    batch, seq, d = x3d.shape
    t = MIX_TILE
    ns = MIX_STREAMS
    nt = seq // t
    assert batch % ns == 0 and seq % t == 0, (batch, seq)
    gw = GROUP_WIDTH
    const_inputs = [consts["dmat"], consts["kw"], consts["qw"], consts["gc"], consts["expand"]]
    in_specs = ([pl.BlockSpec((ns, t, d), lambda b, i: (b, 0, 0)),
                 pl.BlockSpec((ns, t, d), lambda b, i: (b, jnp.minimum(i + 1, nt - 1), 0)),
                 _resident_layer(g_pre_all, l),
                 _resident_layer(w_in_all, l),
                 pl.BlockSpec((t, LANE), lambda b, i: (i, 0)),
                 pl.BlockSpec((t, LANE), lambda b, i: (i, 0))]
                + [_resident(a.shape) for a in const_inputs]
                + [_resident_layer(a, l) for a in layer_inputs])
    y = pl.pallas_call(
        _mix_kernel,
        grid=(batch // ns, nt),
        in_specs=in_specs,
        out_specs=pl.BlockSpec((ns, t, 4 * gw), lambda b, i: (b, i, 0)),
        out_shape=jax.ShapeDtypeStruct((batch, seq, 4 * gw), BF16),
        scratch_shapes=[pltpu.VMEM((2, ns, t, P_PAD), F32),
                        pltpu.VMEM((ns, POOL_HALO + t, gw), F32),
                        pltpu.VMEM((ns, POOL_HALO + t, gw), F32),
                        pltpu.VMEM((ns, POOL_HALO + t, gw), F32),
                        pltpu.VMEM((ns, POOL_HALO + t, LANE), F32),
                        pltpu.VMEM((ns, CONV_HALO + t, 3 * gw), F32),
                        pltpu.VMEM((ns, gw, gw), F32),
                        pltpu.VMEM((ns, gw, gw), F32)],
        compiler_params=pltpu.CompilerParams(dimension_semantics=("arbitrary", "arbitrary"),
                                             vmem_limit_bytes=VMEM_LIMIT),
        name="mixers",
    )(x3d, x3d, g_pre_all, w_in_all, consts["cos"], consts["sin"], *const_inputs, *layer_inputs)
    return y.reshape(batch * seq, 4 * gw)


def _ffn_kernel(y_ref, x_ref, wout_ref, gmix_ref, gpre_ref, wup_ref, cw_ref, cb_ref, wdn_ref, gpost_ref,
                o_ref, gated_ref, ext_ref, carry_ref, *, blocks_per_seq):
    tm = ROW_TILE
    d_ff = wdn_ref.shape[0]
    tf = FF_TILE
    nf = d_ff // tf

    @pl.when((pl.program_id(0) % blocks_per_seq) == 0)
    def _():
        carry_ref[...] = jnp.zeros_like(carry_ref)

    def tile(s):
        rows = slice(s * tm, (s + 1) * tm)
        z_mix = jnp.dot(y_ref[rows, :], wout_ref[...], preferred_element_type=F32)
        x = x_ref[rows, :] + _rms_rows(z_mix) * gmix_ref[...]
        h = (_rms_rows(x) * gpre_ref[...]).astype(BF16)
        yield

        def up(j):
            a = jnp.dot(h, wup_ref[:, j * tf:(j + 1) * tf], preferred_element_type=F32)
            b = jnp.dot(h, wup_ref[:, d_ff + j * tf:d_ff + (j + 1) * tf], preferred_element_type=F32)
            return a, b

        def gate(j, a, b):
            cols = slice(j * tf, (j + 1) * tf)
            slot = j % 2
            ext_ref[s, slot, 0:CONV_HALO, :] = carry_ref[:, cols]
            ext_ref[s, slot, CONV_HALO:CONV_HALO + tm, :] = a
            carry_ref[:, cols] = a[tm - CONV_HALO:, :]
            conv = a * cw_ref[FFN_CONV - 1:FFN_CONV, cols] + cb_ref[:, cols]
            for k in range(1, FFN_CONV):
                conv = conv + (ext_ref[s, slot, pl.ds(CONV_HALO - k, tm), :]
                               * cw_ref[FFN_CONV - 1 - k:FFN_CONV - k, cols])
            return (_gelu(conv) * b).astype(BF16)

        ab = up(0)
        for j in range(nf):
            nxt = up(j + 1) if j + 1 < nf else None
            gated_ref[s, :, j * tf:(j + 1) * tf] = gate(j, *ab)
            ab = nxt
            yield
        z = jnp.dot(gated_ref[s], wdn_ref[...], preferred_element_type=F32)
        yield
        o_ref[rows, :] = x + _rms_rows(z) * gpost_ref[...]

    pending = {s: tile(s) for s in range(FFN_TILES)}
    tick = 0
    while pending:
        for s in sorted(pending):
            if tick >= s * FFN_SKEW and next(pending[s], "done") == "done":
                del pending[s]
        tick += 1


def _ffn(y, x2d, seq, l, w_out, g_mix, g_pre, w_up, conv_w, conv_b, w_down, g_post):
    rows, d = x2d.shape
    d_ff = w_down.shape[1]
    tm = ROW_TILE
    bm = FFN_TILES * tm
    assert rows % bm == 0 and seq % bm == 0, (rows, seq)
    return pl.pallas_call(
        functools.partial(_ffn_kernel, blocks_per_seq=seq // bm),
        grid=(rows // bm,),
        in_specs=[pl.BlockSpec((bm, y.shape[1]), lambda i: (i, 0)),
                  pl.BlockSpec((bm, d), lambda i: (i, 0)),
                  _resident_layer(w_out, l),
                  _resident_layer(g_mix, l),
                  _resident_layer(g_pre, l),
                  _resident_layer(w_up, l),
                  _resident_layer(conv_w, l),
                  _resident_layer(conv_b, l),
                  _resident_layer(w_down, l),
                  _resident_layer(g_post, l)],
        out_specs=pl.BlockSpec((bm, d), lambda i: (i, 0)),
        out_shape=jax.ShapeDtypeStruct((rows, d), F32),
        scratch_shapes=[pltpu.VMEM((FFN_TILES, tm, d_ff), BF16),
                        pltpu.VMEM((FFN_TILES, 2, CONV_HALO + tm, FF_TILE), F32),
                        pltpu.VMEM((CONV_HALO, d_ff), F32)],
        compiler_params=pltpu.CompilerParams(dimension_semantics=("arbitrary",),
                                             vmem_limit_bytes=FFN_VMEM_LIMIT),
        name="conv_ffn",
    )(y, x2d, w_out, g_mix, g_pre, w_up, conv_w, conv_b, w_down, g_post)


def _shape_constants(seq):
    half = HEAD_DIM // 2
    inv = 1.0 / (ROPE_BASE ** (jnp.arange(0, HEAD_DIM, 2, dtype=F32) / HEAD_DIM))
    ang = jnp.arange(seq, dtype=F32)[:, None] * inv[None, :]
    cos, sin = jnp.cos(ang), jnp.sin(ang)
    cos_l = jnp.tile(cos, (1, LANE // half))
    sin_l = jnp.tile(jnp.concatenate([-sin, sin], axis=1), (1, LANE // HEAD_DIM))
    c = RET_CHUNK
    log_gamma = jnp.log(1.0 - 2.0 ** (-5.0 - jnp.arange(N_HEADS, dtype=F32)))
    pos = jnp.arange(c, dtype=F32)
    diff = pos[:, None] - pos[None, :]
    dmat = jnp.where(diff >= 0, jnp.exp(log_gamma[:, None, None] * jnp.maximum(diff, 0.0)), 0.0)
    k_w = jnp.exp(log_gamma[None, :] * (c - 1.0 - pos)[:, None])
    q_w = jnp.exp(log_gamma[None, :] * (pos + 1.0)[:, None])
    g_chunk = jnp.exp(log_gamma * c)[None, :]
    rep = lambda m: jnp.repeat(m, HEAD_DIM, axis=1)
    src = jnp.arange(LANE)[:, None]
    dst = jnp.arange(2 * GROUP_WIDTH)[None, :]
    expand = (src == dst // HEAD_DIM).astype(BF16)
    return {"cos": cos_l, "sin": sin_l, "dmat": dmat.reshape(N_HEADS * c, c), "kw": rep(k_w),
            "qw": rep(q_w), "gc": rep(g_chunk), "expand": expand}


def _stacked_params(pool_w, pool_scale, sgu_ln_g, sgu_ln_b, sgu_ws, sgu_bs, dn_conv_w,
                    dn_a_log, dn_dt_bias, dn_norm_g):
    depth = pool_w.shape[0]
    row = lambda v: v[:, None, :]
    head_eye = jnp.eye(N_HEADS, dtype=F32)
    pool_bd = jnp.einsum("lgcd,gh->lgchd", pool_w, head_eye).reshape(depth, GROUP_WIDTH, GROUP_WIDTH)
    gate_lanes = lambda v: row(jnp.pad(v, ((0, 0), (N_HEADS, LANE - 2 * N_HEADS))))
    return [
        pool_bd.astype(BF16),
        row(pool_scale),
        row(sgu_ln_g),
        row(sgu_ln_b),
        sgu_ws.transpose(0, 2, 1, 3).reshape(depth, SGU_CHUNK, N_HEADS * SGU_CHUNK),
        jnp.repeat(sgu_bs.transpose(0, 2, 1), HEAD_DIM, axis=2),
        dn_conv_w,
        gate_lanes(dn_a_log),
        gate_lanes(dn_dt_bias),
        row(jnp.tile(dn_norm_g, (1, N_HEADS))),
    ]


def kernel(x, norm_pre_mix, norm_post_mix, norm_pre_ffn, norm_post_ffn, w_in, pool_w, pool_scale, sgu_ln_g, sgu_ln_b, sgu_ws, sgu_bs, dn_conv_w, dn_a_log, dn_dt_bias, dn_norm_g, w_out, ffn_w_up, ffn_conv_w, ffn_conv_b, ffn_w_down):
    batch, seq, d = x.shape
    depth = w_in.shape[0]
    consts = _shape_constants(seq)
    w_in_b = jnp.pad(w_in.astype(BF16), ((0, 0), (0, 0), (0, P_PAD - w_in.shape[2])))
    w_out_b = w_out.astype(BF16)
    w_up_b = ffn_w_up.astype(BF16)
    w_down_b = ffn_w_down.astype(BF16)
    mixer_params = _stacked_params(pool_w, pool_scale, sgu_ln_g, sgu_ln_b, sgu_ws, sgu_bs,
                                   dn_conv_w, dn_a_log, dn_dt_bias, dn_norm_g)
    row = lambda v: v[:, None, :]
    g_pre_mix, g_post_mix, g_pre_ffn, g_post_ffn = map(row, (norm_pre_mix, norm_post_mix, norm_pre_ffn, norm_post_ffn))
    conv_b = row(ffn_conv_b)
    x2d = x.reshape(batch * seq, d)
    for l in range(depth):
        y = _mixers(x2d.reshape(batch, seq, d), l, g_pre_mix, w_in_b, consts, mixer_params)
        x2d = _ffn(y, x2d, seq, l, w_out_b, g_post_mix, g_pre_ffn, w_up_b, ffn_conv_w, conv_b, w_down_b, g_post_ffn)
    return x2d.reshape(batch, seq, d)
```
